```python
import math
import jax
import jax.numpy as jnp
from jax import lax
import numpy as np

D_MODEL = 1024
BATCH = 8
SEQ = 4096
DEPTH = 2

GRID_W = 64
CTX_LEN = 256
EPS = 1e-6
A_WIDTH = 512
A_CONV = 3
B_HEADS = 4
B_DK = 128
B_DV = 128
B_CONV = 3
B_CHUNK = 64
B_QK = B_HEADS * B_DK
B_VW = B_HEADS * B_DV
IN_EVEN = 3 * A_WIDTH + 2 * B_QK + 2 * B_VW + 4 * B_HEADS
MIX_EVEN = A_WIDTH + B_VW
C_HEADS = 8
C_Q_LORA = 384
C_KV_LORA = 256
C_NOPE = 128
C_ROPE = 64
C_VDIM = 128
C_QBLOCK = 128
IN_ODD = C_Q_LORA + C_KV_LORA + C_ROPE
MIX_ODD = C_HEADS * C_VDIM
ROPE_BASE = 10000.0
P_HEADS = 8
P_NKEYS = 128
P_NEXP = P_NKEYS * P_NKEYS
P_DKEY = 128
P_TOPK = 16
P_BLOCK = 128
N_EVEN = (DEPTH + 1) // 2
N_ODD = DEPTH // 2

kernel_name = 'hybrid_conv_deltanet_mla_peer_dit'


def rms_norm(x, gain):
    xf = x.astype(jnp.float32)
    y = xf * lax.rsqrt(jnp.mean(xf * xf, axis=-1, keepdims=True) + EPS)
    return (y * gain.astype(jnp.float32)).astype(x.dtype)


def modulate(h, shift, scale):
    return h * (1 + scale[..., None, :]) + shift[..., None, :]


def dwconv_centred(x, w):
    p = w.shape[0] // 2
    return lax.conv_general_dilated(x, w[:, None, :].astype(x.dtype), window_strides=(1,), padding=[(p, p)],
                                    dimension_numbers=('NWC', 'WIO', 'NWC'), feature_group_count=x.shape[-1])


def l2_normalize(t):
    return t * lax.rsqrt(jnp.sum(t * t, axis=-1, keepdims=True) + EPS)


def gdn_chunked(q, k, v, g, beta, s0):
    Bn, H, L, dk = q.shape
    dv = v.shape[-1]
    n = L // B_CHUNK
    q = q * (dk ** -0.5)
    rs = lambda t: t.reshape((Bn, H, n, B_CHUNK) + t.shape[3:])
    q, k, v, g, beta = rs(q), rs(k), rs(v), rs(g), rs(beta)
    decay = jnp.cumsum(g, axis=-1)
    tri = jnp.tril(jnp.ones((B_CHUNK, B_CHUNK), bool))
    strict = jnp.tril(jnp.ones((B_CHUNK, B_CHUNK), bool), -1)
    gamma = jnp.exp(jnp.where(tri, decay[..., :, None] - decay[..., None, :], -jnp.inf))
    kb = k * beta[..., None]
    a = jnp.where(strict, jnp.einsum('bhncd,bhnmd->bhncm', kb, k) * gamma, 0.0)
    eye = jnp.eye(B_CHUNK, dtype=jnp.float32)
    rhs = jnp.concatenate([v * beta[..., None], kb * jnp.exp(decay)[..., None]], axis=-1)
    sol = lax.linalg.triangular_solve(eye + a, rhs, left_side=True, lower=True)
    u, w = sol[..., :dv], sol[..., dv:]
    intra = jnp.where(tri, jnp.einsum('bhncd,bhnmd->bhncm', q, k) * gamma, 0.0)
    q_dec = q * jnp.exp(decay)[..., None]
    k_dec = k * jnp.exp(decay[..., -1:] - decay)[..., None]
    last = jnp.exp(decay[..., -1])

    def step(s, inp):
        u_c, w_c, qd_c, kd_c, intra_c, last_c = inp
        v_new = u_c - jnp.einsum('bhck,bhkv->bhcv', w_c, s)
        o = jnp.einsum('bhck,bhkv->bhcv', qd_c, s) + jnp.einsum('bhcm,bhmv->bhcv', intra_c, v_new)
        s = s * last_c[..., None, None] + jnp.einsum('bhck,bhcv->bhkv', kd_c, v_new)
        return s, o

    xs = tuple(jnp.moveaxis(t, 2, 0) for t in (u, w, q_dec, k_dec, intra, last))
    s_fin, o = lax.scan(step, s0, xs)
    o = jnp.moveaxis(o, 0, 2).reshape(Bn, H, L, dv)
    return o, s_fin


def even_mixer(h, w_in, a_conv, qkv_conv, a_log, dt_bias, o_norm, w_out, s0_f, s0_b):
    Bn, L, _ = h.shape
    proj = h @ w_in
    bg, cg, xa, qkv, z, ab = jnp.split(
        proj, [A_WIDTH, 2 * A_WIDTH, 3 * A_WIDTH, 3 * A_WIDTH + 2 * B_QK + B_VW, 3 * A_WIDTH + 2 * B_QK + 2 * B_VW], axis=-1)
    y_a = bg * dwconv_centred(cg * xa, a_conv)
    qkv = jax.nn.silu(dwconv_centred(qkv, qkv_conv)).astype(jnp.float32)
    q, k, v = jnp.split(qkv, [B_QK, 2 * B_QK], axis=-1)
    heads = lambda t, d: t.reshape(Bn, L, B_HEADS, d).transpose(0, 2, 1, 3)
    q = l2_normalize(heads(q, B_DK))
    k = l2_normalize(heads(k, B_DK))
    v = heads(v, B_DV)
    ab4 = ab.reshape(Bn, L, 4, B_HEADS).transpose(2, 0, 3, 1).astype(jnp.float32)
    a_log = a_log.astype(jnp.float32)
    dt_bias = dt_bias.astype(jnp.float32)
    g_f = -jnp.exp(a_log[0])[None, :, None] * jax.nn.softplus(ab4[0] + dt_bias[0][None, :, None])
    g_b = -jnp.exp(a_log[1])[None, :, None] * jax.nn.softplus(ab4[1] + dt_bias[1][None, :, None])
    beta_f = jax.nn.sigmoid(ab4[2])
    beta_b = jax.nn.sigmoid(ab4[3])
    o_f, s_f = gdn_chunked(q, k, v, g_f, beta_f, s0_f)
    flip = lambda t: jnp.flip(t, axis=2)
    o_b, s_b = gdn_chunked(flip(q), flip(k), flip(v), flip(g_b), flip(beta_b), s0_b)
    o = (o_f + flip(o_b)).transpose(0, 2, 1, 3).astype(h.dtype)
    y_b = (rms_norm(o, o_norm) * jax.nn.silu(z.reshape(Bn, L, B_HEADS, B_DV))).reshape(Bn, L, B_VW)
    y = jnp.concatenate([y_a, y_b], axis=-1) @ w_out
    return y, s_f, s_b


def axial_rope(x, rows, cols):
    half = x.shape[-1] // 2

    def rot(seg, pos):
        n = seg.shape[-1] // 2
        freqs = ROPE_BASE ** (-jnp.arange(n, dtype=jnp.float32) / n)
        ang = pos.astype(jnp.float32)[:, None] * freqs[None, :]
        cos = jnp.cos(ang)[None, :, None, :]
        sin = jnp.sin(ang)[None, :, None, :]
        s1 = seg[..., :n].astype(jnp.float32)
        s2 = seg[..., n:].astype(jnp.float32)
        return jnp.concatenate([s1 * cos - s2 * sin, s2 * cos + s1 * sin], axis=-1)

    return jnp.concatenate([rot(x[..., :half], rows), rot(x[..., half:], cols)], axis=-1).astype(x.dtype)


def mla_project(h, w_in, q_norm, kv_norm, w_uq, w_ukv, rows, cols):
    Bn, L, _ = h.shape
    proj = h @ w_in
    cq, ckv, k_rope = jnp.split(proj, [C_Q_LORA, C_Q_LORA + C_KV_LORA], axis=-1)
    q = (rms_norm(cq, q_norm) @ w_uq).reshape(Bn, L, C_HEADS, C_NOPE + C_ROPE)
    kv = (rms_norm(ckv, kv_norm) @ w_ukv).reshape(Bn, L, C_HEADS, C_NOPE + C_VDIM)
    q_nope, q_rope = q[..., :C_NOPE], q[..., C_NOPE:]
    k_nope, v = kv[..., :C_NOPE], kv[..., C_NOPE:]
    k_rope = k_rope[:, :, None, :]
    if rows is not None:
        q_rope = axial_rope(q_rope, rows, cols)
        k_rope = axial_rope(k_rope, rows, cols)
    q = jnp.concatenate([q_nope, q_rope], axis=-1)
    k = jnp.concatenate([k_nope, jnp.broadcast_to(k_rope, (Bn, L, C_HEADS, C_ROPE))], axis=-1)
    return q, k, v


def block_attention(q, k, v):
    Bn, Lq, H, dq = q.shape
    nb = Lq // C_QBLOCK
    qb = q.reshape(Bn, nb, C_QBLOCK, H, dq).transpose(1, 0, 2, 3, 4)
    scale = dq ** -0.5

    def one(qblk):
        s = jnp.einsum('bqhd,bkhd->bhqk', qblk, k).astype(jnp.float32) * scale
        p = jax.nn.softmax(s, axis=-1).astype(v.dtype)
        return jnp.einsum('bhqk,bkhd->bqhd', p, v)

    o = lax.map(one, qb)
    return o.transpose(1, 0, 2, 3, 4).reshape(Bn, Lq, H, v.shape[-1])


def attn_out(o, w_out):
    Bn, L, H, dv = o.shape
    return o.reshape(Bn, L, H * dv) @ w_out


def peer(h, w_q, sub_keys, u_tab, v_tab):
    Bn, L, D = h.shape
    tok = h.reshape(-1, P_BLOCK, D)

    def one(hb):
        q = (hb @ w_q).reshape(P_BLOCK, P_HEADS, 2, P_DKEY // 2)
        s = jnp.einsum('thpk,hpnk->thpn', q, sub_keys).astype(jnp.float32)
        s1, i1 = lax.top_k(s[:, :, 0], P_TOPK)
        s2, i2 = lax.top_k(s[:, :, 1], P_TOPK)
        cand = (s1[..., :, None] + s2[..., None, :]).reshape(P_BLOCK, P_HEADS, P_TOPK * P_TOPK)
        sc, pos = lax.top_k(cand, P_TOPK)
        idx = (jnp.take_along_axis(i1, pos // P_TOPK, axis=-1) * P_NKEYS
               + jnp.take_along_axis(i2, pos % P_TOPK, axis=-1))
        gate = jax.nn.softmax(sc, axis=-1).astype(hb.dtype)
        act = jax.nn.gelu(jnp.einsum('td,thkd->thk', hb, u_tab[idx]), approximate=False)
        return jnp.einsum('thk,thkd->td', gate * act, v_tab[idx])

    return lax.map(one, tok).reshape(Bn, L, D)


def setup_inputs(seed: int = 0) -> dict:
    key = jax.random.key(seed)
    ks = iter(jax.random.split(key, 40))
    D = D_MODEL
    nrm = lambda shape, s: jax.random.normal(next(ks), shape, jnp.float32) * s
    gain = lambda shape: 1.0 + 0.02 * jax.random.normal(next(ks), shape, jnp.float32)
    x = nrm((BATCH, SEQ, D), 1.0)
    c = nrm((BATCH, D), 1.0)
    ctx = nrm((BATCH, CTX_LEN, D), 1.0)
    c_ctx = nrm((D,), 1.0)
    w_mod = nrm((DEPTH, D, 6 * D), 0.5 * D ** -0.5)
    b_mod = nrm((DEPTH, 6 * D), 0.02)
    norm1 = gain((DEPTH, D))
    norm2 = gain((DEPTH, D))
    ev_w_in = nrm((N_EVEN, D, IN_EVEN), D ** -0.5)
    ev_a_conv = nrm((N_EVEN, A_CONV, A_WIDTH), A_CONV ** -0.5)
    ev_qkv_conv = nrm((N_EVEN, B_CONV, 2 * B_QK + B_VW), B_CONV ** -0.5)
    ev_a_log = jnp.log(jax.random.uniform(next(ks), (N_EVEN, 2, B_HEADS), jnp.float32, 1.0, 16.0))
    dt = jnp.exp(jax.random.uniform(next(ks), (N_EVEN, 2, B_HEADS), jnp.float32, math.log(1e-3), math.log(1e-1)))
    ev_dt_bias = dt + jnp.log(-jnp.expm1(-dt))
    ev_o_norm = gain((N_EVEN, B_DV))
    ev_w_out = nrm((N_EVEN, MIX_EVEN, D), MIX_EVEN ** -0.5)
    od_w_in = nrm((N_ODD, D, IN_ODD), D ** -0.5)
    od_q_norm = gain((N_ODD, C_Q_LORA))
    od_kv_norm = gain((N_ODD, C_KV_LORA))
    od_w_uq = nrm((N_ODD, C_Q_LORA, C_HEADS * (C_NOPE + C_ROPE)), C_Q_LORA ** -0.5)
    od_w_ukv = nrm((N_ODD, C_KV_LORA, C_HEADS * (C_NOPE + C_VDIM)), C_KV_LORA ** -0.5)
    od_w_out = nrm((N_ODD, MIX_ODD, D), MIX_ODD ** -0.5)
    p_w_q = nrm((DEPTH, D, P_HEADS * P_DKEY), D ** -0.5)
    p_keys = nrm((DEPTH, P_HEADS, 2, P_NKEYS, P_DKEY // 2), (P_DKEY // 2) ** -0.5)
    p_u = nrm((DEPTH, P_NEXP, D), D ** -0.5)
    p_v = nrm((DEPTH, P_NEXP, D), 0.5)
    norm_f = gain((D,))
    return {'x': x, 'c': c, 'ctx': ctx, 'c_ctx': c_ctx, 'w_mod': w_mod, 'b_mod': b_mod,
            'norm1': norm1, 'norm2': norm2, 'ev_w_in': ev_w_in, 'ev_a_conv': ev_a_conv,
            'ev_qkv_conv': ev_qkv_conv, 'ev_a_log': ev_a_log, 'ev_dt_bias': ev_dt_bias,
            'ev_o_norm': ev_o_norm, 'ev_w_out': ev_w_out, 'od_w_in': od_w_in, 'od_q_norm': od_q_norm,
            'od_kv_norm': od_kv_norm, 'od_w_uq': od_w_uq, 'od_w_ukv': od_w_ukv, 'od_w_out': od_w_out,
            'p_w_q': p_w_q, 'p_keys': p_keys, 'p_u': p_u, 'p_v': p_v, 'norm_f': norm_f}


def reference(x, c, ctx, c_ctx, w_mod, b_mod, norm1, norm2, ev_w_in, ev_a_conv, ev_qkv_conv, ev_a_log,
              ev_dt_bias, ev_o_norm, ev_w_out, od_w_in, od_q_norm, od_kv_norm, od_w_uq, od_w_ukv, od_w_out,
              p_w_q, p_keys, p_u, p_v, norm_f):
    Bn, S, _ = x.shape
    n_rows = S // GRID_W
    rows = jnp.repeat(jnp.arange(n_rows, dtype=jnp.int32), GRID_W)
    cols = jnp.tile(jnp.arange(GRID_W, dtype=jnp.int32), n_rows)
    for i in range(DEPTH):
        last = i == DEPTH - 1
        j = i // 2
        mod_x = jnp.split(jax.nn.silu(c) @ w_mod[i] + b_mod[i], 6, axis=-1)
        mod_c = jnp.split(jax.nn.silu(c_ctx) @ w_mod[i] + b_mod[i], 6, axis=-1)
        hx = modulate(rms_norm(x, norm1[i]), mod_x[0], mod_x[1])
        hc = modulate(rms_norm(ctx, norm1[i]), mod_c[0], mod_c[1])
        if i % 2 == 0:
            zero = jnp.zeros((Bn, B_HEADS, B_DK, B_DV), jnp.float32)
            yc, s_f, s_b = even_mixer(hc, ev_w_in[j], ev_a_conv[j], ev_qkv_conv[j], ev_a_log[j], ev_dt_bias[j],
                                      ev_o_norm[j], ev_w_out[j], zero, zero)
            yx, _, _ = even_mixer(hx, ev_w_in[j], ev_a_conv[j], ev_qkv_conv[j], ev_a_log[j], ev_dt_bias[j],
                                  ev_o_norm[j], ev_w_out[j], s_f, s_b)
        else:
            qc, kc, vc = mla_project(hc, od_w_in[j], od_q_norm[j], od_kv_norm[j], od_w_uq[j], od_w_ukv[j], None, None)
            qx, kx, vx = mla_project(hx, od_w_in[j], od_q_norm[j], od_kv_norm[j], od_w_uq[j], od_w_ukv[j], rows, cols)
            yx = attn_out(block_attention(qx, jnp.concatenate([kx, kc], axis=1), jnp.concatenate([vx, vc], axis=1)),
                          od_w_out[j])
            yc = None if last else attn_out(block_attention(qc, kc, vc), od_w_out[j])
        x = x + mod_x[2][:, None, :] * yx
        hx2 = modulate(rms_norm(x, norm2[i]), mod_x[3], mod_x[4])
        x = x + mod_x[5][:, None, :] * peer(hx2, p_w_q[i], p_keys[i], p_u[i], p_v[i])
        if not last:
            ctx = ctx + mod_c[2] * yc
            hc2 = modulate(rms_norm(ctx, norm2[i]), mod_c[3], mod_c[4])
            ctx = ctx + mod_c[5] * peer(hc2, p_w_q[i], p_keys[i], p_u[i], p_v[i])
    return rms_norm(x, norm_f)
```

```python
import functools
import math

import jax
import jax.numpy as jnp
from jax import lax
from jax.experimental import pallas as pl
from jax.experimental.pallas import tpu as pltpu

F32 = jnp.float32
BF16 = jnp.bfloat16
HIGHEST = lax.Precision.HIGHEST

EPS = 1e-6
GRID_W = 64
ROPE_BASE = 10000.0
TM = 256
GDN_C = 128
GDN_HEADS = 4
C_HEADS = 8
C_NOPE = 128
C_ROPE = 64
P_HEADS = 8
P_NKEYS = 128
P_TOPK = 16
P_PAIRS = P_HEADS * P_TOPK
PEER_TB = 128
TOPK_TB = 256
ATTN_TQ = 256
VMEM_LIMIT = 48 * 1024 * 1024
PEER_VMEM = 50 * 1024 * 1024


def _nt_dot(a, b, precision=None):
    return lax.dot_general(a, b, (((1,), (1,)), ((), ())), precision=precision, preferred_element_type=F32)


def _tn_dot(a, b, precision=None):
    return lax.dot_general(a, b, (((0,), (0,)), ((), ())), precision=precision, preferred_element_type=F32)


def _dot(a, b, precision=None):
    return jnp.dot(a, b, precision=precision, preferred_element_type=F32)


def _silu(x):
    return x * jax.nn.sigmoid(x)


def _params(*sem):
    return pltpu.CompilerParams(dimension_semantics=sem, vmem_limit_bytes=VMEM_LIMIT)


def _full(shape):
    return pl.BlockSpec(shape, lambda *_: (0,) * len(shape))


def _mod_kernel(c_ref, w_ref, b_ref, o_ref):
    o_ref[...] = _dot(_silu(c_ref[...]), w_ref[...], HIGHEST) + b_ref[...]


def _modulation(c_rows, w, b):
    r, d = c_rows.shape
    n = w.shape[1]
    tn = 512
    out = pl.pallas_call(
        _mod_kernel,
        grid=(n // tn,),
        in_specs=[_full((r, d)), pl.BlockSpec((d, tn), lambda j: (0, j)), pl.BlockSpec((1, tn), lambda j: (0, j))],
        out_specs=pl.BlockSpec((r, tn), lambda j: (0, j)),
        out_shape=jax.ShapeDtypeStruct((r, n), F32),
        compiler_params=_params("arbitrary"),
        name="modulation",
    )(c_rows, w, b.reshape(1, n))
    return out.reshape(r, 6, d)


def _proj_kernel(*refs, shift_row, highest, has_wt, emit_h):
    x_ref, g_ref, m_ref, w_ref = refs[:4]
    rest = list(refs[4:])
    wt_ref = rest.pop(0) if has_wt else None
    o_ref = rest.pop(0)
    x = x_ref[0]
    h = x * lax.rsqrt(jnp.mean(x * x, axis=-1, keepdims=True) + EPS) * g_ref[...]
    h = h * (1.0 + m_ref[0, shift_row + 1:shift_row + 2, :]) + m_ref[0, shift_row:shift_row + 1, :]
    if highest:
        o_ref[0] = _dot(h, w_ref[...], HIGHEST).astype(o_ref.dtype)
    else:
        o_ref[0] = _dot(h.astype(BF16), w_ref[...]).astype(o_ref.dtype)
    if has_wt:
        rest.pop(0)[0] = _nt_dot(wt_ref[...], h, HIGHEST)
    if emit_h:
        rest.pop(0)[0] = h


def _norm_mod_proj(x3, gain, mods, mod_index, shift_row, w, *, highest=False, w_t=None, emit_h=False, name):
    b, l, d = x3.shape
    n = w.shape[1]
    grid = (b, l // TM)
    in_specs = [pl.BlockSpec((1, TM, d), lambda bi, i: (bi, i, 0)),
                _full((1, d)),
                pl.BlockSpec((1, 6, d), lambda bi, i: (mod_index(bi, i), 0, 0)),
                _full(w.shape)]
    args = [x3, gain.reshape(1, d), mods, w]
    out_specs = [pl.BlockSpec((1, TM, n), lambda bi, i: (bi, i, 0))]
    out_shape = [jax.ShapeDtypeStruct((b, l, n), F32)]
    if w_t is not None:
        in_specs.append(_full(w_t.shape))
        args.append(w_t)
        out_specs.append(pl.BlockSpec((1, w_t.shape[0], TM), lambda bi, i: (bi, 0, i)))
        out_shape.append(jax.ShapeDtypeStruct((b, w_t.shape[0], l), F32))
    if emit_h:
        out_specs.append(pl.BlockSpec((1, TM, d), lambda bi, i: (bi, i, 0)))
        out_shape.append(jax.ShapeDtypeStruct((b, l, d), F32))
    return pl.pallas_call(
        functools.partial(_proj_kernel, shift_row=shift_row, highest=highest, has_wt=w_t is not None, emit_h=emit_h),
        grid=grid, in_specs=in_specs, out_specs=out_specs, out_shape=out_shape,
        compiler_params=_params("arbitrary", "arbitrary"), name=name,
    )(*args)


def _even_prep_kernel(p_ref, halo_ref, aconv_ref, qconv_ref, ya_ref, qkv_ref):
    tm = p_ref.shape[1]
    rows = lax.broadcasted_iota(jnp.int32, (tm, 512), 0)
    first = rows == 0
    last = rows == tm - 1

    def conv(x, x_prev, x_next, w):
        xm = jnp.where(first, x_prev, pltpu.roll(x, 1, 0))
        xp = jnp.where(last, x_next, pltpu.roll(x, tm - 1, 0))
        return xm * w[0:1] + x * w[1:2] + xp * w[2:3]

    def cols(k):
        return p_ref[0, :, 512 * k:512 * (k + 1)]

    def halo(r, k):
        return halo_ref[0, 0, r:r + 1, 512 * k:512 * (k + 1)]

    ca = conv(cols(1) * cols(2), halo(0, 1) * halo(0, 2), halo(1, 1) * halo(1, 2), aconv_ref[...])
    ya_ref[0] = cols(0) * ca
    for s in range(3):
        y = _silu(conv(cols(3 + s), halo(0, 3 + s), halo(1, 3 + s), qconv_ref[:, 512 * s:512 * (s + 1)]))
        if s < 2:
            parts = []
            for h in range(GDN_HEADS):
                t = y[:, 128 * h:128 * (h + 1)]
                parts.append(t * lax.rsqrt(jnp.sum(t * t, axis=-1, keepdims=True) + EPS))
            y = jnp.concatenate(parts, axis=-1)
        qkv_ref[0, :, 512 * s:512 * (s + 1)] = y


def _even_prep(proj, a_conv, qkv_conv):
    b, l, n = proj.shape
    nb = l // TM
    lastrows = proj[:, TM - 1::TM, :]
    firstrows = proj[:, 0::TM, :]
    blk = jnp.arange(nb)
    prev = jnp.where((blk >= 2)[None, :, None], jnp.roll(lastrows, 1, axis=1), 0.0)
    nxt = jnp.where(((blk >= 1) & (blk < nb - 1))[None, :, None], jnp.roll(firstrows, -1, axis=1), 0.0)
    halo = jnp.stack([prev, nxt], axis=2)
    return pl.pallas_call(
        _even_prep_kernel,
        grid=(b, nb),
        in_specs=[pl.BlockSpec((1, TM, 3072), lambda bi, i: (bi, i, 0)),
                  pl.BlockSpec((1, 1, 2, n), lambda bi, i: (bi, i, 0, 0)),
                  _full(a_conv.shape), _full(qkv_conv.shape)],
        out_specs=[pl.BlockSpec((1, TM, 512), lambda bi, i: (bi, i, 0)),
                   pl.BlockSpec((1, TM, 1536), lambda bi, i: (bi, i, 0))],
        out_shape=[jax.ShapeDtypeStruct((b, l, 512), F32), jax.ShapeDtypeStruct((b, l, 1536), F32)],
        compiler_params=_params("arbitrary", "arbitrary"), name="even_prep",
    )(proj, halo, a_conv, qkv_conv)


def _gdn_kernel(qf_ref, qb_ref, abf_ref, abb_ref, alog_ref, dtb_ref, of_ref, ob_ref, s_ref):
    n = pl.program_id(1)
    c = GDN_C

    @pl.when(n == 0)
    def _():
        s_ref[...] = jnp.zeros_like(s_ref)

    row = lax.broadcasted_iota(jnp.int32, (c, c), 0)
    col = lax.broadcasted_iota(jnp.int32, (c, c), 1)
    eye = row == col
    eyef = jnp.where(eye, 1.0, 0.0)
    ones = jnp.ones((c, c), F32)
    scale = 128 ** -0.5
    for d, (x_ref, ab_ref, o_ref) in enumerate(((qf_ref, abf_ref, of_ref), (qb_ref, abb_ref, ob_ref))):
        tri = (row >= col) if d == 0 else (row <= col)
        strict = (row > col) if d == 0 else (row < col)
        trif = jnp.where(tri, 1.0, 0.0)
        ab = ab_ref[0]
        g4 = -jnp.exp(alog_ref[4 * d:4 * d + 4, :]) * jax.nn.softplus(ab[4 * d:4 * d + 4, :] + dtb_ref[4 * d:4 * d + 4, :])
        beta4 = jax.nn.sigmoid(ab[8 + 4 * d:12 + 4 * d, :])
        for h in range(GDN_HEADS):
            gt = jnp.broadcast_to(g4[h:h + 1, :], (c, c))
            bt = jnp.broadcast_to(beta4[h:h + 1, :], (c, c))
            dcol = _dot(jnp.where(tri, gt, 0.0), ones, HIGHEST)
            drow = _nt_dot(gt, trif, HIGHEST)
            tot = _dot(gt, ones, HIGHEST)
            bc = _dot(jnp.where(eye, bt, 0.0), ones, HIGHEST)
            gamma = jnp.exp(jnp.where(tri, dcol - drow, -jnp.inf))
            edec = jnp.exp(dcol)
            q = x_ref[0, :, 128 * h:128 * (h + 1)] * scale
            k = x_ref[0, :, 512 + 128 * h:512 + 128 * (h + 1)]
            v = x_ref[0, :, 1024 + 128 * h:1024 + 128 * (h + 1)]
            kb = k * bc
            a = jnp.where(strict, _nt_dot(kb, k, HIGHEST) * gamma, 0.0)
            t = eyef - a
            p = a
            for _ in range(6):
                p = _dot(p, p, HIGHEST)
                t = t + _dot(t, p, HIGHEST)
            u = _dot(t, v * bc, HIGHEST)
            w = _dot(t, kb * edec, HIGHEST)
            intra = jnp.where(tri, _nt_dot(q, k, HIGHEST) * gamma, 0.0)
            s = s_ref[4 * d + h]
            v_new = u - _dot(w, s, HIGHEST)
            o = _dot(q * edec, s, HIGHEST) + _dot(intra, v_new, HIGHEST)
            s_ref[4 * d + h] = s * jnp.exp(tot) + _tn_dot(k * jnp.exp(tot - dcol), v_new, HIGHEST)
            o_ref[0, :, 128 * h:128 * (h + 1)] = o


def _gdn(qkv, ab_t, a_log, dt_bias):
    b, l, _ = qkv.shape
    nb = l // GDN_C
    nctx = TM // GDN_C

    def bwd(n):
        return jnp.where(n < nctx, nctx - 1 - n, nb + nctx - 1 - n)

    alog = jnp.broadcast_to(a_log.reshape(8, 1), (8, 128))
    dtb = jnp.broadcast_to(dt_bias.reshape(8, 1), (8, 128))
    return pl.pallas_call(
        _gdn_kernel,
        grid=(b, nb),
        in_specs=[pl.BlockSpec((1, GDN_C, 1536), lambda bi, n: (bi, n, 0)),
                  pl.BlockSpec((1, GDN_C, 1536), lambda bi, n: (bi, bwd(n), 0)),
                  pl.BlockSpec((1, 16, GDN_C), lambda bi, n: (bi, 0, n)),
                  pl.BlockSpec((1, 16, GDN_C), lambda bi, n: (bi, 0, bwd(n))),
                  _full((8, 128)), _full((8, 128))],
        out_specs=[pl.BlockSpec((1, GDN_C, 512), lambda bi, n: (bi, n, 0)),
                   pl.BlockSpec((1, GDN_C, 512), lambda bi, n: (bi, bwd(n), 0))],
        out_shape=[jax.ShapeDtypeStruct((b, l, 512), F32), jax.ShapeDtypeStruct((b, l, 512), F32)],
        scratch_shapes=[pltpu.VMEM((2 * GDN_HEADS, 128, 128), F32)],
        compiler_params=_params("arbitrary", "arbitrary"), name="gdn",
    )(qkv, qkv, ab_t, ab_t, alog, dtb)


def _even_out_kernel(of_ref, ob_ref, z_ref, ya_ref, x_ref, m_ref, on_ref, w_ref, o_ref):
    o = of_ref[0] + ob_ref[0]
    z = z_ref[0]
    parts = []
    for h in range(GDN_HEADS):
        oh = o[:, 128 * h:128 * (h + 1)]
        y = oh * lax.rsqrt(jnp.mean(oh * oh, axis=-1, keepdims=True) + EPS) * on_ref[...]
        parts.append(y * _silu(z[:, 128 * h:128 * (h + 1)]))
    yb = jnp.concatenate(parts, axis=-1)
    y = _dot(ya_ref[0].astype(BF16), w_ref[0:512, :]) + _dot(yb.astype(BF16), w_ref[512:1536, :])
    o_ref[0] = x_ref[0] + m_ref[0, 2:3, :] * y


def _even_out(o_f, o_b, proj, y_a, x3, mods, mod_index, o_norm, w_out):
    b, l, d = x3.shape
    tok = lambda n: pl.BlockSpec((1, TM, n), lambda bi, i: (bi, i, 0))
    return pl.pallas_call(
        _even_out_kernel,
        grid=(b, l // TM),
        in_specs=[tok(512), tok(512),
                  pl.BlockSpec((1, TM, 512), lambda bi, i: (bi, i, 6)),
                  tok(512), tok(d),
                  pl.BlockSpec((1, 6, d), lambda bi, i: (mod_index(bi, i), 0, 0)),
                  _full((1, 128)), _full(w_out.shape)],
        out_specs=tok(d),
        out_shape=jax.ShapeDtypeStruct((b, l, d), F32),
        compiler_params=_params("arbitrary", "arbitrary"), name="even_out",
    )(o_f, o_b, proj, y_a, x3, mods, o_norm.reshape(1, 128), w_out.astype(BF16))


def _rope_tables(s):
    t = jnp.arange(s, dtype=jnp.int32)
    n = C_ROPE // 4
    freqs = ROPE_BASE ** (-jnp.arange(n, dtype=F32) / n)

    def seg(pos):
        ang = pos.astype(F32)[:, None] * freqs[None, :]
        return (jnp.concatenate([jnp.cos(ang), jnp.cos(ang)], -1), jnp.concatenate([-jnp.sin(ang), jnp.sin(ang)], -1))

    cr, sr = seg(t // GRID_W)
    cc, sc = seg(t % GRID_W)
    cos = jnp.concatenate([jnp.ones((TM, C_ROPE), F32), jnp.concatenate([cr, cc], -1)], 0)
    sin = jnp.concatenate([jnp.zeros((TM, C_ROPE), F32), jnp.concatenate([sr, sc], -1)], 0)
    return cos, sin


def _swap_cols(w):
    n = C_ROPE // 4
    perm = jnp.concatenate([jnp.arange(n) + n, jnp.arange(n), jnp.arange(n) + 3 * n, jnp.arange(n) + 2 * n])
    return w[..., perm]


def _pad_lanes(w, width=128):
    return jnp.concatenate([w, jnp.zeros(w.shape[:-1] + (width - w.shape[-1],), w.dtype)], axis=-1)


def _mla_expand_kernel(p_ref, qn_ref, kvn_ref, cos_ref, sin_ref, wqn_ref, wqr_ref, wqs_ref, wk_ref, wv_ref,
                       oqn_ref, oqr_ref, okn_ref, okr_ref, ov_ref):
    def norm(x, g):
        return (x * lax.rsqrt(jnp.mean(x * x, axis=-1, keepdims=True) + EPS) * g).astype(BF16)

    cq = norm(p_ref[0, :, 0:384], qn_ref[...])
    ckv = norm(p_ref[0, :, 384:640], kvn_ref[...])
    cos = cos_ref[...]
    sin = sin_ref[...]
    scale = (C_NOPE + C_ROPE) ** -0.5
    oqn_ref[0] = (_dot(cq, wqn_ref[...]) * scale).astype(BF16)
    cos8 = jnp.concatenate([cos] * C_HEADS, axis=-1)
    sin8 = jnp.concatenate([sin] * C_HEADS, axis=-1)
    oqr_ref[0] = ((_dot(cq, wqr_ref[...]) * cos8 + _dot(cq, wqs_ref[...]) * sin8) * scale).astype(BF16)
    okn_ref[0] = _dot(ckv, wk_ref[...]).astype(BF16)
    ov_ref[0] = _dot(ckv, wv_ref[...]).astype(BF16)
    okr_ref[0] = (p_ref[0, :, 640:768] * cos + p_ref[0, :, 768:896] * sin).astype(BF16)


def _mla_expand(proj, q_norm, kv_norm, w_uq, w_ukv, cos, sin):
    b, l, n = proj.shape
    hd = C_HEADS * 128
    wq = w_uq.reshape(-1, C_HEADS, C_NOPE + C_ROPE)
    wqn = wq[:, :, :C_NOPE].reshape(-1, hd).astype(BF16)
    wqr = _pad_lanes(wq[:, :, C_NOPE:]).reshape(-1, hd).astype(BF16)
    wqs = _pad_lanes(_swap_cols(wq[:, :, C_NOPE:])).reshape(-1, hd).astype(BF16)
    wkv = w_ukv.reshape(-1, C_HEADS, 2 * 128)
    wk = wkv[:, :, :128].reshape(-1, hd).astype(BF16)
    wv = wkv[:, :, 128:].reshape(-1, hd).astype(BF16)
    tok = lambda m: pl.BlockSpec((1, TM, m), lambda bi, i: (bi, i, 0))
    tab = pl.BlockSpec((TM, 128), lambda bi, i: (i, 0))
    return pl.pallas_call(
        _mla_expand_kernel,
        grid=(b, l // TM),
        in_specs=[tok(n), _full((1, 384)), _full((1, 256)), tab, tab,
                  _full(wqn.shape), _full(wqr.shape), _full(wqs.shape), _full(wk.shape), _full(wv.shape)],
        out_specs=[tok(hd), tok(hd), tok(hd), tok(128), tok(hd)],
        out_shape=[jax.ShapeDtypeStruct((b, l, hd), BF16), jax.ShapeDtypeStruct((b, l, hd), BF16),
                   jax.ShapeDtypeStruct((b, l, hd), BF16), jax.ShapeDtypeStruct((b, l, 128), BF16),
                   jax.ShapeDtypeStruct((b, l, hd), BF16)],
        compiler_params=_params("arbitrary", "arbitrary"), name="mla_expand",
    )(proj, q_norm.reshape(1, -1), kv_norm.reshape(1, -1), _pad_lanes(cos), _pad_lanes(sin), wqn, wqr, wqs, wk, wv)


def _attn_kernel(qn_ref, qr_ref, kn_ref, kr_ref, v_ref, o_ref):
    s = _nt_dot(qn_ref[0], kn_ref[0]) + _nt_dot(qr_ref[0], kr_ref[0])
    m = jnp.max(s, axis=-1, keepdims=True)
    p = jnp.exp(s - m)
    den = jnp.sum(p, axis=-1, keepdims=True)
    o_ref[0] = (_dot(p.astype(BF16), v_ref[0]) / den).astype(o_ref.dtype)


def _attention(qn, qr, kn, kr, v):
    b, l, hd = qn.shape
    s = l - TM
    q_spec = pl.BlockSpec((1, ATTN_TQ, 128), lambda bi, h, i: (bi, i + TM // ATTN_TQ, h))
    kv_spec = pl.BlockSpec((1, l, 128), lambda bi, h, i: (bi, 0, h))
    return pl.pallas_call(
        _attn_kernel,
        grid=(b, C_HEADS, s // ATTN_TQ),
        in_specs=[q_spec, q_spec, kv_spec, pl.BlockSpec((1, l, 128), lambda bi, h, i: (bi, 0, 0)), kv_spec],
        out_specs=pl.BlockSpec((1, ATTN_TQ, 128), lambda bi, h, i: (bi, i, h)),
        out_shape=jax.ShapeDtypeStruct((b, s, hd), BF16),
        compiler_params=_params("arbitrary", "arbitrary", "arbitrary"), name="attention",
    )(qn, qr, kn, kr, v)


def _attn_out_kernel(o_ref, x_ref, m_ref, w_ref, out_ref):
    out_ref[0] = x_ref[0] + m_ref[0, 2:3, :] * _dot(o_ref[0], w_ref[...])


def _attn_out(o, x3, mods, w_out):
    b, s, d = o.shape[0], o.shape[1], x3.shape[2]
    return pl.pallas_call(
        _attn_out_kernel,
        grid=(b, s // TM),
        in_specs=[pl.BlockSpec((1, TM, o.shape[2]), lambda bi, i: (bi, i, 0)),
                  pl.BlockSpec((1, TM, d), lambda bi, i: (bi, i + 1, 0)),
                  pl.BlockSpec((1, 6, d), lambda bi, i: (bi, 0, 0)),
                  _full(w_out.shape)],
        out_specs=pl.BlockSpec((1, TM, d), lambda bi, i: (bi, i, 0)),
        out_shape=jax.ShapeDtypeStruct((b, s, d), F32),
        compiler_params=_params("arbitrary", "arbitrary"), name="attn_out",
    )(o, x3, mods, w_out.astype(BF16))


def _topk_rows(s, k):
    n = s.shape[0]
    iota = lax.broadcasted_iota(jnp.int32, s.shape, 0)
    vals, idxs = [], []
    for _ in range(k):
        m = jnp.max(s, axis=0, keepdims=True)
        am = jnp.min(jnp.where(s == m, iota, n), axis=0, keepdims=True)
        vals.append(m)
        idxs.append(am)
        s = jnp.where(iota == am, -jnp.inf, s)
    return vals, idxs


def _peer_topk_kernel(q_ref, key_ref, idx_ref, gate_ref):
    tl = 128
    for sub in range(q_ref.shape[0] // tl):
        for h in range(P_HEADS):
            q = q_ref[sub * tl:(sub + 1) * tl, 128 * h:128 * (h + 1)]
            st = _nt_dot(key_ref[h], q, HIGHEST)
            v1, i1 = _topk_rows(st[:P_NKEYS], P_TOPK)
            v2, i2 = _topk_rows(st[P_NKEYS:], P_TOPK)
            v2s = jnp.concatenate(v2, axis=0)
            cand = jnp.concatenate([v1[a] + v2s for a in range(P_TOPK)], axis=0)
            sc, pos = _topk_rows(cand, P_TOPK)
            sc = jnp.concatenate(sc, axis=0)
            pos = jnp.concatenate(pos, axis=0)
            pa = pos >> 4
            pb = pos & (P_TOPK - 1)
            e1 = jnp.zeros_like(pos)
            e2 = jnp.zeros_like(pos)
            for r in range(P_TOPK):
                e1 = jnp.where(pa == r, i1[r], e1)
                e2 = jnp.where(pb == r, i2[r], e2)
            ex = jnp.exp(sc - sc[0:1])
            idx_ref[P_TOPK * h:P_TOPK * (h + 1), sub * tl:(sub + 1) * tl] = e1 * P_NKEYS + e2
            gate_ref[P_TOPK * h:P_TOPK * (h + 1), sub * tl:(sub + 1) * tl] = ex / jnp.sum(ex, axis=0, keepdims=True)


def _peer_topk(q, keys):
    t = q.shape[0]
    z = jnp.zeros((P_HEADS, P_NKEYS, 64), F32)
    kbd = jnp.concatenate([jnp.concatenate([keys[:, 0], z], -1), jnp.concatenate([z, keys[:, 1]], -1)], axis=1)
    return pl.pallas_call(
        _peer_topk_kernel,
        grid=(t // TOPK_TB,),
        in_specs=[pl.BlockSpec((TOPK_TB, 1024), lambda i: (i, 0)), _full(kbd.shape)],
        out_specs=[pl.BlockSpec((P_PAIRS, TOPK_TB), lambda i: (0, i)), pl.BlockSpec((P_PAIRS, TOPK_TB), lambda i: (0, i))],
        out_shape=[jax.ShapeDtypeStruct((P_PAIRS, t), jnp.int32), jax.ShapeDtypeStruct((P_PAIRS, t), F32)],
        compiler_params=_params("arbitrary"), name="peer_topk",
    )(q, kbd)


def _pack_table(tab):
    e = tab.shape[0]
    tb = lax.bitcast_convert_type(tab.astype(BF16), jnp.uint16).astype(jnp.uint32).reshape(e * 4, 2, 128)
    word = tb[:, 0, :] | (tb[:, 1, :] << 16)
    return lax.bitcast_convert_type(word, jnp.int32)


def _diag_mask():
    lane = lax.broadcasted_iota(jnp.int32, (8, 8 * P_PAIRS), 1)
    sub = lax.broadcasted_iota(jnp.int32, (8, 8 * P_PAIRS), 0)
    return (lane & 7) == sub


def _gather_rows(idx_ref, tab_ref, t, g_ref):
    for j in range(P_PAIRS):
        e = idx_ref[t, j]
        g_ref[pl.ds(4 * j, 4), :] = tab_ref[pl.ds(pl.multiple_of(e * 4, 4), 4), :]


def _peer_act_kernel(idx_ref, h_ref, gate_ref, tab_ref, w_ref, g0, g1, r_ref):
    tb = h_ref.shape[0]
    diag = _diag_mask()

    def score(t, g_ref):
        gb = pltpu.bitcast(g_ref[...], BF16)
        hb = h_ref[t].astype(BF16)
        r = _nt_dot(hb, gb)
        r_ref[pl.ds(pl.multiple_of(t * 8, 8), 8), :] = jnp.where(diag, r, 0.0)

    _gather_rows(idx_ref, tab_ref, 0, g0)

    def body(i, c):
        t = 2 * i
        _gather_rows(idx_ref, tab_ref, t + 1, g1)
        score(t, g0)
        _gather_rows(idx_ref, tab_ref, jnp.minimum(t + 2, tb - 1), g0)
        score(t + 1, g1)
        return c

    lax.fori_loop(0, tb // 2, body, 0)

    r = r_ref[...]
    hi = r.astype(BF16)
    lo = (r - hi.astype(F32)).astype(BF16)
    col = lax.broadcasted_iota(jnp.int32, (8 * P_PAIRS, P_PAIRS), 0)
    pair = lax.broadcasted_iota(jnp.int32, (8 * P_PAIRS, P_PAIRS), 1)
    fold = jnp.where((col >> 3) == pair, 1.0, 0.0).astype(BF16)
    a8 = _dot(hi, fold) + _dot(lo, fold)
    act = jnp.sum(a8.reshape(tb, 8, P_PAIRS), axis=1)
    gelu = 0.5 * act * (1.0 + lax.erf(act * (1.0 / math.sqrt(2.0))))
    w_ref[...] = gate_ref[...] * gelu


def _peer_out_kernel(idx_ref, w_ref, tab_ref, o_ref, g0, g1, wexp_ref):
    tb = w_ref.shape[0]
    diag = _diag_mask()
    pair = lax.broadcasted_iota(jnp.int32, (P_PAIRS, 8 * P_PAIRS), 0)
    col = lax.broadcasted_iota(jnp.int32, (P_PAIRS, 8 * P_PAIRS), 1)
    spread = jnp.where((col >> 3) == pair, 1.0, 0.0).astype(BF16)
    wexp_ref[...] = _dot(w_ref[...].astype(BF16), spread)

    def combine(t, g_ref):
        gb = pltpu.bitcast(g_ref[...], BF16)
        wrow = jnp.broadcast_to(wexp_ref[pl.ds(t, 1), :], (8, 8 * P_PAIRS))
        wd = jnp.where(diag, wrow, 0.0).astype(BF16)
        o_ref[t] = _dot(wd, gb)

    _gather_rows(idx_ref, tab_ref, 0, g0)

    def body(i, c):
        t = 2 * i
        _gather_rows(idx_ref, tab_ref, t + 1, g1)
        combine(t, g0)
        _gather_rows(idx_ref, tab_ref, jnp.minimum(t + 2, tb - 1), g0)
        combine(t + 1, g1)
        return c

    lax.fori_loop(0, tb // 2, body, 0)


def _peer_experts(h, idx, gate, u_pack, v_pack):
    t = h.shape[0]
    tb = PEER_TB
    h3 = h.reshape(t, 8, 128)
    grid = (t // tb,)
    idx_spec = pl.BlockSpec((tb, P_PAIRS), lambda i: (i, 0), memory_space=pltpu.SMEM)
    tab_spec = pl.BlockSpec(memory_space=pltpu.VMEM)
    params = pltpu.CompilerParams(dimension_semantics=("arbitrary",), vmem_limit_bytes=PEER_VMEM)
    w = pl.pallas_call(
        _peer_act_kernel,
        grid=grid,
        in_specs=[idx_spec,
                  pl.BlockSpec((tb, 8, 128), lambda i: (i, 0, 0)),
                  pl.BlockSpec((tb, P_PAIRS), lambda i: (i, 0)),
                  tab_spec],
        out_specs=pl.BlockSpec((tb, P_PAIRS), lambda i: (i, 0)),
        out_shape=jax.ShapeDtypeStruct((t, P_PAIRS), F32),
        scratch_shapes=[pltpu.VMEM((4 * P_PAIRS, 128), jnp.int32), pltpu.VMEM((4 * P_PAIRS, 128), jnp.int32),
                        pltpu.VMEM((8 * tb, 8 * P_PAIRS), F32)],
        compiler_params=params,
        name="peer_act",
    )(idx, h3, gate, u_pack)
    o3 = pl.pallas_call(
        _peer_out_kernel,
        grid=grid,
        in_specs=[idx_spec,
                  pl.BlockSpec((tb, P_PAIRS), lambda i: (i, 0)),
                  tab_spec],
        out_specs=pl.BlockSpec((tb, 8, 128), lambda i: (i, 0, 0)),
        out_shape=jax.ShapeDtypeStruct((t, 8, 128), F32),
        scratch_shapes=[pltpu.VMEM((4 * P_PAIRS, 128), jnp.int32), pltpu.VMEM((4 * P_PAIRS, 128), jnp.int32),
                        pltpu.VMEM((tb, 8 * P_PAIRS), F32)],
        compiler_params=params,
        name="peer_out",
    )(idx, w, v_pack)
    return o3.reshape(t, 1024)


def _resid_kernel(x_ref, p_ref, m_ref, nf_ref, o_ref, *, final):
    y = x_ref[0] + m_ref[0, 5:6, :] * p_ref[0]
    if final:
        y = y * lax.rsqrt(jnp.mean(y * y, axis=-1, keepdims=True) + EPS) * nf_ref[...]
    o_ref[0] = y


def _peer_layer(x3, norm2, mods, mod_index, w_q, keys, u_tab, v_tab, norm_f, final):
    b, l, d = x3.shape
    q, h = _norm_mod_proj(x3, norm2, mods, mod_index, 3, w_q, highest=True, emit_h=True, name="peer_query")
    idx_t, gate_t = _peer_topk(q.reshape(b * l, d), keys)
    y = _peer_experts(h.reshape(b * l, d), idx_t.T, gate_t.T, _pack_table(u_tab), _pack_table(v_tab))
    tok = pl.BlockSpec((1, TM, d), lambda bi, i: (bi, i, 0))
    return pl.pallas_call(
        functools.partial(_resid_kernel, final=final),
        grid=(b, l // TM),
        in_specs=[tok, tok, pl.BlockSpec((1, 6, d), lambda bi, i: (mod_index(bi, i), 0, 0)), _full((1, d))],
        out_specs=tok,
        out_shape=jax.ShapeDtypeStruct((b, l, d), F32),
        compiler_params=_params("arbitrary", "arbitrary"), name="peer_resid",
    )(x3, y.reshape(b, l, d), mods, norm_f.reshape(1, d))


def kernel(x, c, ctx, c_ctx, w_mod, b_mod, norm1, norm2, ev_w_in, ev_a_conv, ev_qkv_conv, ev_a_log, ev_dt_bias, ev_o_norm, ev_w_out, od_w_in, od_q_norm, od_kv_norm, od_w_uq, od_w_ukv, od_w_out, p_w_q, p_keys, p_u, p_v, norm_f):
    b, s, d = x.shape
    assert ctx.shape[1] == TM and s % TM == 0
    c_rows = jnp.concatenate([c, c_ctx[None, :], jnp.zeros((16 - b - 1, d), F32)], axis=0)
    seq = jnp.concatenate([ctx, x], axis=1)
    joint = lambda bi, i: jnp.where(i == 0, b, bi)
    latent = lambda bi, i: bi

    mods = _modulation(c_rows, w_mod[0], b_mod[0])
    w_in = ev_w_in[0]
    proj, ab_t = _norm_mod_proj(seq, norm1[0], mods, joint, 0, w_in[:, :3584].astype(BF16),
                                w_t=w_in[:, 3584:].T, name="even_in")
    y_a, qkv = _even_prep(proj, ev_a_conv[0], ev_qkv_conv[0])
    o_f, o_b = _gdn(qkv, ab_t, ev_a_log[0], ev_dt_bias[0])
    seq = _even_out(o_f, o_b, proj, y_a, seq, mods, joint, ev_o_norm[0], ev_w_out[0])
    seq = _peer_layer(seq, norm2[0], mods, joint, p_w_q[0], p_keys[0], p_u[0], p_v[0], norm_f, False)

    mods = _modulation(c_rows, w_mod[1], b_mod[1])
    w_in = od_w_in[0]
    k_rope = w_in[:, 640:704]
    w_ext = jnp.concatenate([w_in[:, :640], _pad_lanes(k_rope), _pad_lanes(_swap_cols(k_rope))], axis=1)
    (proj,) = _norm_mod_proj(seq, norm1[1], mods, joint, 0, w_ext.astype(BF16), name="mla_in")
    cos, sin = _rope_tables(s)
    qn, qr, kn, kr, v = _mla_expand(proj, od_q_norm[0], od_kv_norm[0], od_w_uq[0], od_w_ukv[0], cos, sin)
    o = _attention(qn, qr, kn, kr, v)
    xs = _attn_out(o, seq, mods, od_w_out[0])
    return _peer_layer(xs, norm2[1], mods, latent, p_w_q[1], p_keys[1], p_u[1], p_v[1], norm_f, True)
```

```python
import functools
import math

import jax
import jax.numpy as jnp
from jax import lax
from jax.experimental import pallas as pl
from jax.experimental.pallas import tpu as pltpu

F32 = jnp.float32
BF16 = jnp.bfloat16
HIGHEST = lax.Precision.HIGHEST

EPS = 1e-6
GRID_W = 64
ROPE_BASE = 10000.0
TM = 256
GDN_C = 128
GDN_HEADS = 4
C_HEADS = 8
C_NOPE = 128
C_ROPE = 64
P_HEADS = 8
P_NKEYS = 128
P_TOPK = 16
P_PAIRS = P_HEADS * P_TOPK
PEER_NT = 32
TOPK_TB = 256
ATTN_TQ = 256
VMEM_LIMIT = 48 * 1024 * 1024
PEER_VMEM = 50 * 1024 * 1024


def _nt_dot(a, b, precision=None):
    return lax.dot_general(a, b, (((1,), (1,)), ((), ())), precision=precision, preferred_element_type=F32)


def _tn_dot(a, b, precision=None):
    return lax.dot_general(a, b, (((0,), (0,)), ((), ())), precision=precision, preferred_element_type=F32)


def _dot(a, b, precision=None):
    return jnp.dot(a, b, precision=precision, preferred_element_type=F32)


_NN = (((1,), (0,)), ((), ()))
_NT = (((1,), (1,)), ((), ()))
_TN = (((0,), (0,)), ((), ()))


def _split2(x):
    hi = x.astype(BF16)
    return hi, (x - hi.astype(F32)).astype(BF16)


def _mm3(a, b, dims):
    f = lambda x, y: lax.dot_general(x, y, dims, preferred_element_type=F32)
    return f(a[0], b[0]) + (f(a[0], b[1]) + f(a[1], b[0]))


def _mm_exact(x, m, dims):
    hi = x.astype(BF16)
    r = x - hi.astype(F32)
    mid = r.astype(BF16)
    lo = (r - mid.astype(F32)).astype(BF16)
    f = lambda y: lax.dot_general(y, m, dims, preferred_element_type=F32)
    return f(hi) + (f(mid) + f(lo))


def _silu(x):
    return x * jax.nn.sigmoid(x)


def _params(*sem):
    return pltpu.CompilerParams(dimension_semantics=sem, vmem_limit_bytes=VMEM_LIMIT)


def _full(shape):
    return pl.BlockSpec(shape, lambda *_: (0,) * len(shape))


def _mod_kernel(c_ref, w_ref, b_ref, o_ref):
    o_ref[...] = _dot(_silu(c_ref[...]), w_ref[...], HIGHEST) + b_ref[...]


def _modulation(c_rows, w, b):
    r, d = c_rows.shape
    n = w.shape[1]
    tn = 512
    out = pl.pallas_call(
        _mod_kernel,
        grid=(n // tn,),
        in_specs=[_full((r, d)), pl.BlockSpec((d, tn), lambda j: (0, j)), pl.BlockSpec((1, tn), lambda j: (0, j))],
        out_specs=pl.BlockSpec((r, tn), lambda j: (0, j)),
        out_shape=jax.ShapeDtypeStruct((r, n), F32),
        compiler_params=_params("arbitrary"),
        name="modulation",
    )(c_rows, w, b.reshape(1, n))
    return out.reshape(r, 6, d)


def _proj_kernel(*refs, shift_row, highest, has_wt, emit_h):
    x_ref, g_ref, m_ref, w_ref = refs[:4]
    rest = list(refs[4:])
    wt_ref = rest.pop(0) if has_wt else None
    o_ref = rest.pop(0)
    x = x_ref[0]
    h = x * lax.rsqrt(jnp.mean(x * x, axis=-1, keepdims=True) + EPS) * g_ref[...]
    h = h * (1.0 + m_ref[0, shift_row + 1:shift_row + 2, :]) + m_ref[0, shift_row:shift_row + 1, :]
    if highest:
        o_ref[0] = _dot(h, w_ref[...], HIGHEST).astype(o_ref.dtype)
    else:
        o_ref[0] = _dot(h.astype(BF16), w_ref[...]).astype(o_ref.dtype)
    if has_wt:
        rest.pop(0)[0] = _nt_dot(wt_ref[...], h, HIGHEST)
    if emit_h:
        rest.pop(0)[0] = h


def _norm_mod_proj(x3, gain, mods, mod_index, shift_row, w, *, highest=False, w_t=None, emit_h=False, name):
    b, l, d = x3.shape
    n = w.shape[1]
    grid = (b, l // TM)
    in_specs = [pl.BlockSpec((1, TM, d), lambda bi, i: (bi, i, 0)),
                _full((1, d)),
                pl.BlockSpec((1, 6, d), lambda bi, i: (mod_index(bi, i), 0, 0)),
                _full(w.shape)]
    args = [x3, gain.reshape(1, d), mods, w]
    out_specs = [pl.BlockSpec((1, TM, n), lambda bi, i: (bi, i, 0))]
    out_shape = [jax.ShapeDtypeStruct((b, l, n), F32)]
    if w_t is not None:
        in_specs.append(_full(w_t.shape))
        args.append(w_t)
        out_specs.append(pl.BlockSpec((1, w_t.shape[0], TM), lambda bi, i: (bi, 0, i)))
        out_shape.append(jax.ShapeDtypeStruct((b, w_t.shape[0], l), F32))
    if emit_h:
        out_specs.append(pl.BlockSpec((1, TM, d), lambda bi, i: (bi, i, 0)))
        out_shape.append(jax.ShapeDtypeStruct((b, l, d), F32))
    return pl.pallas_call(
        functools.partial(_proj_kernel, shift_row=shift_row, highest=highest, has_wt=w_t is not None, emit_h=emit_h),
        grid=grid, in_specs=in_specs, out_specs=out_specs, out_shape=out_shape,
        compiler_params=_params("arbitrary", "arbitrary"), name=name,
    )(*args)


def _even_prep_kernel(p_ref, halo_ref, aconv_ref, qconv_ref, ya_ref, qkv_ref):
    tm = p_ref.shape[1]
    rows = lax.broadcasted_iota(jnp.int32, (tm, 512), 0)
    first = rows == 0
    last = rows == tm - 1

    def conv(x, x_prev, x_next, w):
        xm = jnp.where(first, x_prev, pltpu.roll(x, 1, 0))
        xp = jnp.where(last, x_next, pltpu.roll(x, tm - 1, 0))
        return xm * w[0:1] + x * w[1:2] + xp * w[2:3]

    def cols(k):
        return p_ref[0, :, 512 * k:512 * (k + 1)]

    def halo(r, k):
        return halo_ref[0, 0, r:r + 1, 512 * k:512 * (k + 1)]

    ca = conv(cols(1) * cols(2), halo(0, 1) * halo(0, 2), halo(1, 1) * halo(1, 2), aconv_ref[...])
    ya_ref[0] = cols(0) * ca
    for s in range(3):
        y = _silu(conv(cols(3 + s), halo(0, 3 + s), halo(1, 3 + s), qconv_ref[:, 512 * s:512 * (s + 1)]))
        if s < 2:
            parts = []
            for h in range(GDN_HEADS):
                t = y[:, 128 * h:128 * (h + 1)]
                parts.append(t * lax.rsqrt(jnp.sum(t * t, axis=-1, keepdims=True) + EPS))
            y = jnp.concatenate(parts, axis=-1)
        qkv_ref[0, :, 512 * s:512 * (s + 1)] = y


def _even_prep(proj, a_conv, qkv_conv):
    b, l, n = proj.shape
    nb = l // TM
    lastrows = proj[:, TM - 1::TM, :]
    firstrows = proj[:, 0::TM, :]
    blk = jnp.arange(nb)
    prev = jnp.where((blk >= 2)[None, :, None], jnp.roll(lastrows, 1, axis=1), 0.0)
    nxt = jnp.where(((blk >= 1) & (blk < nb - 1))[None, :, None], jnp.roll(firstrows, -1, axis=1), 0.0)
    halo = jnp.stack([prev, nxt], axis=2)
    return pl.pallas_call(
        _even_prep_kernel,
        grid=(b, nb),
        in_specs=[pl.BlockSpec((1, TM, 3072), lambda bi, i: (bi, i, 0)),
                  pl.BlockSpec((1, 1, 2, n), lambda bi, i: (bi, i, 0, 0)),
                  _full(a_conv.shape), _full(qkv_conv.shape)],
        out_specs=[pl.BlockSpec((1, TM, 512), lambda bi, i: (bi, i, 0)),
                   pl.BlockSpec((1, TM, 1536), lambda bi, i: (bi, i, 0))],
        out_shape=[jax.ShapeDtypeStruct((b, l, 512), F32), jax.ShapeDtypeStruct((b, l, 1536), F32)],
        compiler_params=_params("arbitrary", "arbitrary"), name="even_prep",
    )(proj, halo, a_conv, qkv_conv)


def _gdn_kernel(qf_ref, qb_ref, abf_ref, abb_ref, alog_ref, dtb_ref, of_ref, ob_ref, s_ref):
    n = pl.program_id(1)
    c = GDN_C

    @pl.when(n == 0)
    def _():
        s_ref[...] = jnp.zeros_like(s_ref)

    row = lax.broadcasted_iota(jnp.int32, (c, c), 0)
    col = lax.broadcasted_iota(jnp.int32, (c, c), 1)
    eye = row == col
    eyef = jnp.where(eye, 1.0, 0.0)
    ones = jnp.ones((c, c), BF16)
    scale = 128 ** -0.5
    for d, (x_ref, ab_ref, o_ref) in enumerate(((qf_ref, abf_ref, of_ref), (qb_ref, abb_ref, ob_ref))):
        tri = (row >= col) if d == 0 else (row <= col)
        strict = (row > col) if d == 0 else (row < col)
        trif = jnp.where(tri, 1.0, 0.0).astype(BF16)
        ab = ab_ref[0]
        g4 = -jnp.exp(alog_ref[4 * d:4 * d + 4, :]) * jax.nn.softplus(ab[4 * d:4 * d + 4, :] + dtb_ref[4 * d:4 * d + 4, :])
        beta4 = jax.nn.sigmoid(ab[8 + 4 * d:12 + 4 * d, :])
        for h in range(GDN_HEADS):
            gt = jnp.broadcast_to(g4[h:h + 1, :], (c, c))
            bt = jnp.broadcast_to(beta4[h:h + 1, :], (c, c))
            dcol = _mm_exact(jnp.where(tri, gt, 0.0), ones, _NN)
            drow = _mm_exact(gt, trif, _NT)
            tot = _mm_exact(gt, ones, _NN)
            bc = _mm_exact(jnp.where(eye, bt, 0.0), ones, _NN)
            gamma = jnp.exp(jnp.where(tri, dcol - drow, -jnp.inf))
            edec = jnp.exp(dcol)
            q = x_ref[0, :, 128 * h:128 * (h + 1)] * scale
            k = x_ref[0, :, 512 + 128 * h:512 + 128 * (h + 1)]
            v = x_ref[0, :, 1024 + 128 * h:1024 + 128 * (h + 1)]
            kb = k * bc
            k2 = _split2(k)
            a = jnp.where(strict, _mm3(_split2(kb), k2, _NT) * gamma, 0.0)
            t = eyef - a
            p2 = _split2(a)
            for _ in range(6):
                p2 = _split2(_mm3(p2, p2, _NN))
                t = t + _mm3(_split2(t), p2, _NN)
            t2 = _split2(t)
            u = _mm3(t2, _split2(v * bc), _NN)
            w = _mm3(t2, _split2(kb * edec), _NN)
            intra = jnp.where(tri, _mm3(_split2(q), k2, _NT) * gamma, 0.0)
            s = s_ref[4 * d + h]
            s2 = _split2(s)
            v_new = u - _mm3(_split2(w), s2, _NN)
            v2 = _split2(v_new)
            o = _mm3(_split2(q * edec), s2, _NN) + _mm3(_split2(intra), v2, _NN)
            s_ref[4 * d + h] = s * jnp.exp(tot) + _mm3(_split2(k * jnp.exp(tot - dcol)), v2, _TN)
            o_ref[0, :, 128 * h:128 * (h + 1)] = o


def _gdn(qkv, ab_t, a_log, dt_bias):
    b, l, _ = qkv.shape
    nb = l // GDN_C
    nctx = TM // GDN_C

    def bwd(n):
        return jnp.where(n < nctx, nctx - 1 - n, nb + nctx - 1 - n)

    alog = jnp.broadcast_to(a_log.reshape(8, 1), (8, 128))
    dtb = jnp.broadcast_to(dt_bias.reshape(8, 1), (8, 128))
    return pl.pallas_call(
        _gdn_kernel,
        grid=(b, nb),
        in_specs=[pl.BlockSpec((1, GDN_C, 1536), lambda bi, n: (bi, n, 0)),
                  pl.BlockSpec((1, GDN_C, 1536), lambda bi, n: (bi, bwd(n), 0)),
                  pl.BlockSpec((1, 16, GDN_C), lambda bi, n: (bi, 0, n)),
                  pl.BlockSpec((1, 16, GDN_C), lambda bi, n: (bi, 0, bwd(n))),
                  _full((8, 128)), _full((8, 128))],
        out_specs=[pl.BlockSpec((1, GDN_C, 512), lambda bi, n: (bi, n, 0)),
                   pl.BlockSpec((1, GDN_C, 512), lambda bi, n: (bi, bwd(n), 0))],
        out_shape=[jax.ShapeDtypeStruct((b, l, 512), F32), jax.ShapeDtypeStruct((b, l, 512), F32)],
        scratch_shapes=[pltpu.VMEM((2 * GDN_HEADS, 128, 128), F32)],
        compiler_params=_params("arbitrary", "arbitrary"), name="gdn",
    )(qkv, qkv, ab_t, ab_t, alog, dtb)


def _even_out_kernel(of_ref, ob_ref, z_ref, ya_ref, x_ref, m_ref, on_ref, w_ref, o_ref):
    o = of_ref[0] + ob_ref[0]
    z = z_ref[0]
    parts = []
    for h in range(GDN_HEADS):
        oh = o[:, 128 * h:128 * (h + 1)]
        y = oh * lax.rsqrt(jnp.mean(oh * oh, axis=-1, keepdims=True) + EPS) * on_ref[...]
        parts.append(y * _silu(z[:, 128 * h:128 * (h + 1)]))
    yb = jnp.concatenate(parts, axis=-1)
    y = _dot(ya_ref[0].astype(BF16), w_ref[0:512, :]) + _dot(yb.astype(BF16), w_ref[512:1536, :])
    o_ref[0] = x_ref[0] + m_ref[0, 2:3, :] * y


def _even_out(o_f, o_b, proj, y_a, x3, mods, mod_index, o_norm, w_out):
    b, l, d = x3.shape
    tok = lambda n: pl.BlockSpec((1, TM, n), lambda bi, i: (bi, i, 0))
    return pl.pallas_call(
        _even_out_kernel,
        grid=(b, l // TM),
        in_specs=[tok(512), tok(512),
                  pl.BlockSpec((1, TM, 512), lambda bi, i: (bi, i, 6)),
                  tok(512), tok(d),
                  pl.BlockSpec((1, 6, d), lambda bi, i: (mod_index(bi, i), 0, 0)),
                  _full((1, 128)), _full(w_out.shape)],
        out_specs=tok(d),
        out_shape=jax.ShapeDtypeStruct((b, l, d), F32),
        compiler_params=_params("arbitrary", "arbitrary"), name="even_out",
    )(o_f, o_b, proj, y_a, x3, mods, o_norm.reshape(1, 128), w_out.astype(BF16))


def _rope_tables(s):
    t = jnp.arange(s, dtype=jnp.int32)
    n = C_ROPE // 4
    freqs = ROPE_BASE ** (-jnp.arange(n, dtype=F32) / n)

    def seg(pos):
        ang = pos.astype(F32)[:, None] * freqs[None, :]
        return (jnp.concatenate([jnp.cos(ang), jnp.cos(ang)], -1), jnp.concatenate([-jnp.sin(ang), jnp.sin(ang)], -1))

    cr, sr = seg(t // GRID_W)
    cc, sc = seg(t % GRID_W)
    cos = jnp.concatenate([jnp.ones((TM, C_ROPE), F32), jnp.concatenate([cr, cc], -1)], 0)
    sin = jnp.concatenate([jnp.zeros((TM, C_ROPE), F32), jnp.concatenate([sr, sc], -1)], 0)
    return cos, sin


def _swap_cols(w):
    n = C_ROPE // 4
    perm = jnp.concatenate([jnp.arange(n) + n, jnp.arange(n), jnp.arange(n) + 3 * n, jnp.arange(n) + 2 * n])
    return w[..., perm]


def _pad_lanes(w, width=128):
    return jnp.concatenate([w, jnp.zeros(w.shape[:-1] + (width - w.shape[-1],), w.dtype)], axis=-1)


def _mla_expand_kernel(p_ref, qn_ref, kvn_ref, cos_ref, sin_ref, wqn_ref, wqr_ref, wqs_ref, wk_ref, wv_ref,
                       oqn_ref, oqr_ref, okn_ref, okr_ref, ov_ref):
    def norm(x, g):
        return (x * lax.rsqrt(jnp.mean(x * x, axis=-1, keepdims=True) + EPS) * g).astype(BF16)

    cq = norm(p_ref[0, :, 0:384], qn_ref[...])
    ckv = norm(p_ref[0, :, 384:640], kvn_ref[...])
    cos = cos_ref[...]
    sin = sin_ref[...]
    scale = (C_NOPE + C_ROPE) ** -0.5
    oqn_ref[0] = (_dot(cq, wqn_ref[...]) * scale).astype(BF16)
    cos8 = jnp.concatenate([cos] * C_HEADS, axis=-1)
    sin8 = jnp.concatenate([sin] * C_HEADS, axis=-1)
    oqr_ref[0] = ((_dot(cq, wqr_ref[...]) * cos8 + _dot(cq, wqs_ref[...]) * sin8) * scale).astype(BF16)
    okn_ref[0] = _dot(ckv, wk_ref[...]).astype(BF16)
    ov_ref[0] = _dot(ckv, wv_ref[...]).astype(BF16)
    okr_ref[0] = (p_ref[0, :, 640:768] * cos + p_ref[0, :, 768:896] * sin).astype(BF16)


def _mla_expand(proj, q_norm, kv_norm, w_uq, w_ukv, cos, sin):
    b, l, n = proj.shape
    hd = C_HEADS * 128
    wq = w_uq.reshape(-1, C_HEADS, C_NOPE + C_ROPE)
    wqn = wq[:, :, :C_NOPE].reshape(-1, hd).astype(BF16)
    wqr = _pad_lanes(wq[:, :, C_NOPE:]).reshape(-1, hd).astype(BF16)
    wqs = _pad_lanes(_swap_cols(wq[:, :, C_NOPE:])).reshape(-1, hd).astype(BF16)
    wkv = w_ukv.reshape(-1, C_HEADS, 2 * 128)
    wk = wkv[:, :, :128].reshape(-1, hd).astype(BF16)
    wv = wkv[:, :, 128:].reshape(-1, hd).astype(BF16)
    tok = lambda m: pl.BlockSpec((1, TM, m), lambda bi, i: (bi, i, 0))
    tab = pl.BlockSpec((TM, 128), lambda bi, i: (i, 0))
    return pl.pallas_call(
        _mla_expand_kernel,
        grid=(b, l // TM),
        in_specs=[tok(n), _full((1, 384)), _full((1, 256)), tab, tab,
                  _full(wqn.shape), _full(wqr.shape), _full(wqs.shape), _full(wk.shape), _full(wv.shape)],
        out_specs=[tok(hd), tok(hd), tok(hd), tok(128), tok(hd)],
        out_shape=[jax.ShapeDtypeStruct((b, l, hd), BF16), jax.ShapeDtypeStruct((b, l, hd), BF16),
                   jax.ShapeDtypeStruct((b, l, hd), BF16), jax.ShapeDtypeStruct((b, l, 128), BF16),
                   jax.ShapeDtypeStruct((b, l, hd), BF16)],
        compiler_params=_params("arbitrary", "arbitrary"), name="mla_expand",
    )(proj, q_norm.reshape(1, -1), kv_norm.reshape(1, -1), _pad_lanes(cos), _pad_lanes(sin), wqn, wqr, wqs, wk, wv)


def _attn_kernel(qn_ref, qr_ref, kn_ref, kr_ref, v_ref, o_ref):
    s = _nt_dot(qn_ref[0], kn_ref[0]) + _nt_dot(qr_ref[0], kr_ref[0])
    m = jnp.max(s, axis=-1, keepdims=True)
    p = jnp.exp(s - m)
    den = jnp.sum(p, axis=-1, keepdims=True)
    o_ref[0] = (_dot(p.astype(BF16), v_ref[0]) / den).astype(o_ref.dtype)


def _attention(qn, qr, kn, kr, v):
    b, l, hd = qn.shape
    s = l - TM
    q_spec = pl.BlockSpec((1, ATTN_TQ, 128), lambda bi, h, i: (bi, i + TM // ATTN_TQ, h))
    kv_spec = pl.BlockSpec((1, l, 128), lambda bi, h, i: (bi, 0, h))
    return pl.pallas_call(
        _attn_kernel,
        grid=(b, C_HEADS, s // ATTN_TQ),
        in_specs=[q_spec, q_spec, kv_spec, pl.BlockSpec((1, l, 128), lambda bi, h, i: (bi, 0, 0)), kv_spec],
        out_specs=pl.BlockSpec((1, ATTN_TQ, 128), lambda bi, h, i: (bi, i, h)),
        out_shape=jax.ShapeDtypeStruct((b, s, hd), BF16),
        compiler_params=_params("arbitrary", "arbitrary", "arbitrary"), name="attention",
    )(qn, qr, kn, kr, v)


def _attn_out_kernel(o_ref, x_ref, m_ref, w_ref, out_ref):
    out_ref[0] = x_ref[0] + m_ref[0, 2:3, :] * _dot(o_ref[0], w_ref[...])


def _attn_out(o, x3, mods, w_out):
    b, s, d = o.shape[0], o.shape[1], x3.shape[2]
    return pl.pallas_call(
        _attn_out_kernel,
        grid=(b, s // TM),
        in_specs=[pl.BlockSpec((1, TM, o.shape[2]), lambda bi, i: (bi, i, 0)),
                  pl.BlockSpec((1, TM, d), lambda bi, i: (bi, i + 1, 0)),
                  pl.BlockSpec((1, 6, d), lambda bi, i: (bi, 0, 0)),
                  _full(w_out.shape)],
        out_specs=pl.BlockSpec((1, TM, d), lambda bi, i: (bi, i, 0)),
        out_shape=jax.ShapeDtypeStruct((b, s, d), F32),
        compiler_params=_params("arbitrary", "arbitrary"), name="attn_out",
    )(o, x3, mods, w_out.astype(BF16))


def _topk_rows(s, k, ids=None):
    if ids is None:
        ids = lax.broadcasted_iota(jnp.int32, s.shape, 0)
    big = jnp.iinfo(jnp.int32).max
    vals, idxs = [], []
    for _ in range(k):
        m = jnp.max(s, axis=0, keepdims=True)
        am = jnp.min(jnp.where(s == m, ids, big), axis=0, keepdims=True)
        vals.append(m)
        idxs.append(am)
        s = jnp.where(ids == am, -jnp.inf, s)
    return vals, idxs


def _candidates(v1, v2):
    tl = v1[0].shape[1]
    v2s = jnp.concatenate(v2, axis=0)
    sub8 = lax.broadcasted_iota(jnp.int32, (8, tl), 0)
    sub16 = lax.broadcasted_iota(jnp.int32, (P_TOPK, tl), 0)
    vals = [v1[0] + v2s]
    ids = [sub16]
    for a in range(1, 5):
        nb = P_TOPK // (a + 1)
        vals.append(jnp.where(sub8 < nb, v1[a] + v2s[:8], -jnp.inf))
        ids.append(sub8 + P_TOPK * a)
    rest = [(a, b) for a in range(5, P_TOPK) for b in range(P_TOPK // (a + 1))]
    for group in (rest[:8], rest[8:]):
        rows = [v1[a] + v2[b] for a, b in group]
        rid = [jnp.full((1, tl), P_TOPK * a + b, jnp.int32) for a, b in group]
        pad = 8 - len(group)
        if pad:
            rows.append(jnp.full((pad, tl), -jnp.inf, F32))
            rid.append(jnp.full((pad, tl), P_TOPK * P_TOPK, jnp.int32))
        vals.append(jnp.concatenate(rows, axis=0))
        ids.append(jnp.concatenate(rid, axis=0))
    return jnp.concatenate(vals, axis=0), jnp.concatenate(ids, axis=0)


def _peer_topk_kernel(q_ref, key_ref, idx_ref, gate_ref):
    tl = 128
    for sub in range(q_ref.shape[0] // tl):
        for h in range(P_HEADS):
            q = q_ref[sub * tl:(sub + 1) * tl, 128 * h:128 * (h + 1)]
            st = _nt_dot(key_ref[h], q, HIGHEST)
            v1, i1 = _topk_rows(st[:P_NKEYS], P_TOPK)
            v2, i2 = _topk_rows(st[P_NKEYS:], P_TOPK)
            cand, cand_pos = _candidates(v1, v2)
            sc, pos = _topk_rows(cand, P_TOPK, cand_pos)
            sc = jnp.concatenate(sc, axis=0)
            pos = jnp.concatenate(pos, axis=0)
            pa = pos >> 4
            pb = pos & (P_TOPK - 1)
            e1 = jnp.zeros_like(pos)
            e2 = jnp.zeros_like(pos)
            for r in range(P_TOPK):
                e1 = jnp.where(pa == r, i1[r], e1)
                e2 = jnp.where(pb == r, i2[r], e2)
            ex = jnp.exp(sc - sc[0:1])
            idx_ref[P_TOPK * h:P_TOPK * (h + 1), sub * tl:(sub + 1) * tl] = (e1 * P_NKEYS + e2) * 4
            gate_ref[P_TOPK * h:P_TOPK * (h + 1), sub * tl:(sub + 1) * tl] = ex / jnp.sum(ex, axis=0, keepdims=True)


def _peer_topk(q, keys):
    t = q.shape[0]
    z = jnp.zeros((P_HEADS, P_NKEYS, 64), F32)
    kbd = jnp.concatenate([jnp.concatenate([keys[:, 0], z], -1), jnp.concatenate([z, keys[:, 1]], -1)], axis=1)
    return pl.pallas_call(
        _peer_topk_kernel,
        grid=(t // TOPK_TB,),
        in_specs=[pl.BlockSpec((TOPK_TB, 1024), lambda i: (i, 0)), _full(kbd.shape)],
        out_specs=[pl.BlockSpec((P_PAIRS, TOPK_TB), lambda i: (0, i)), pl.BlockSpec((P_PAIRS, TOPK_TB), lambda i: (0, i))],
        out_shape=[jax.ShapeDtypeStruct((P_PAIRS, t), jnp.int32), jax.ShapeDtypeStruct((P_PAIRS, t), F32)],
        compiler_params=_params("arbitrary"), name="peer_topk",
    )(q, kbd)


def _pack_table(tab):
    e = tab.shape[0]
    tb = lax.bitcast_convert_type(tab.astype(BF16), jnp.uint16).astype(jnp.uint32).reshape(e * 4, 2, 128)
    word = tb[:, 0, :] | (tb[:, 1, :] << 16)
    return lax.bitcast_convert_type(word, jnp.int32)


def _gather_rows(idx_smem, tab_ref, tt, g_ref):
    for j in range(P_PAIRS):
        row = idx_smem[tt * P_PAIRS + j]
        g_ref[pl.ds(4 * j, 4), :] = tab_ref[pl.ds(pl.multiple_of(row, 4), 4), :]


def _index_stream(idx_hbm, bufs, sems, per_token):
    i = pl.program_id(0)
    n = pl.num_programs(0)

    def chunk(c, k):
        return pltpu.make_async_copy(idx_hbm.at[c], bufs[k], sems.at[k])

    @pl.when(i == 0)
    def _():
        chunk(0, 0).start()

    chunk(2 * i + 1, 1).start()
    chunk(2 * i, 0).wait()
    for tt in range(PEER_NT):
        per_token(tt, tt, bufs[0])

    @pl.when(i + 1 < n)
    def _():
        chunk(2 * i + 2, 0).start()

    chunk(2 * i + 1, 1).wait()
    for tt in range(PEER_NT):
        per_token(PEER_NT + tt, tt, bufs[1])


def _peer_act_kernel(idx_hbm, h_ref, gate_ref, tab_ref, w_ref, ia, ib, sems, g0, g1, rlo_ref, rhi_ref):
    half = 4 * P_PAIRS
    lane = lax.broadcasted_iota(jnp.int32, (16, half), 1)
    sub = lax.broadcasted_iota(jnp.int32, (16, half), 0)
    diag = (lane & 7) == (sub & 7)
    zero = jnp.zeros((8, 128), BF16)
    gbufs = (g0, g1)

    def per_token(t, tt, idx_smem):
        g_ref = gbufs[tt % 2]
        _gather_rows(idx_smem, tab_ref, tt, g_ref)
        gb = pltpu.bitcast(g_ref[...], BF16)
        g2 = jnp.concatenate([gb[:half], gb[half:]], axis=1)
        hb = h_ref[t].astype(BF16)
        lhs = jnp.concatenate([jnp.concatenate([hb, zero], axis=1), jnp.concatenate([zero, hb], axis=1)], axis=0)
        r = jnp.where(diag, _nt_dot(lhs, g2), 0.0)
        rlo_ref[t:t + 1, :] = jnp.sum(r[:8], axis=0, keepdims=True)
        rhi_ref[t:t + 1, :] = jnp.sum(r[8:], axis=0, keepdims=True)

    _index_stream(idx_hbm, (ia, ib), sems, per_token)

    col = lax.broadcasted_iota(jnp.int32, (half, P_PAIRS), 0)
    pair = lax.broadcasted_iota(jnp.int32, (half, P_PAIRS), 1)

    def fold(r, first_pair):
        f = jnp.where((col >> 3) + first_pair == pair, 1.0, 0.0).astype(BF16)
        hi = r.astype(BF16)
        lo = (r - hi.astype(F32)).astype(BF16)
        return _dot(hi, f) + _dot(lo, f)

    act = fold(rlo_ref[...], 0) + fold(rhi_ref[...], P_PAIRS // 2)
    gelu = 0.5 * act * (1.0 + lax.erf(act * (1.0 / math.sqrt(2.0))))
    w_ref[...] = gate_ref[...] * gelu


def _peer_out_kernel(idx_hbm, w_ref, tab_ref, o_ref, ia, ib, sems, g0, g1, whi_ref, wlo_ref):
    lane = lax.broadcasted_iota(jnp.int32, (8, 8 * P_PAIRS), 1)
    sub = lax.broadcasted_iota(jnp.int32, (8, 8 * P_PAIRS), 0)
    diag = (lane & 7) == sub
    pair = lax.broadcasted_iota(jnp.int32, (P_PAIRS, 8 * P_PAIRS), 0)
    col = lax.broadcasted_iota(jnp.int32, (P_PAIRS, 8 * P_PAIRS), 1)
    spread = jnp.where((col >> 3) == pair, 1.0, 0.0).astype(BF16)
    w = w_ref[...]
    w_hi = w.astype(BF16)
    w_lo = (w - w_hi.astype(F32)).astype(BF16)
    whi_ref[...] = _dot(w_hi, spread)
    wlo_ref[...] = _dot(w_lo, spread)
    gbufs = (g0, g1)

    def per_token(t, tt, idx_smem):
        g_ref = gbufs[tt % 2]
        _gather_rows(idx_smem, tab_ref, tt, g_ref)
        gb = pltpu.bitcast(g_ref[...], BF16)
        rows = [jnp.where(diag, jnp.broadcast_to(r[t:t + 1, :], (8, 8 * P_PAIRS)), 0.0).astype(BF16)
                for r in (whi_ref, wlo_ref)]
        o = _dot(jnp.concatenate(rows, axis=0), gb)
        o_ref[t] = o[:8] + o[8:]

    _index_stream(idx_hbm, (ia, ib), sems, per_token)


def _peer_experts(h, idx4, gate, u_pack, v_pack):
    t = h.shape[0]
    tb = 2 * PEER_NT
    h3 = h.reshape(t, 8, 128)
    idx_chunks = idx4.reshape(t // PEER_NT, PEER_NT * P_PAIRS)
    grid = (t // tb,)
    idx_spec = pl.BlockSpec(memory_space=pl.ANY)
    tab_spec = pl.BlockSpec(memory_space=pltpu.VMEM)
    params = pltpu.CompilerParams(dimension_semantics=("arbitrary",), vmem_limit_bytes=PEER_VMEM)
    stream_scratch = [pltpu.SMEM((PEER_NT * P_PAIRS,), jnp.int32), pltpu.SMEM((PEER_NT * P_PAIRS,), jnp.int32),
                      pltpu.SemaphoreType.DMA((2,)),
                      pltpu.VMEM((4 * P_PAIRS, 128), jnp.int32), pltpu.VMEM((4 * P_PAIRS, 128), jnp.int32)]
    w = pl.pallas_call(
        _peer_act_kernel,
        grid=grid,
        in_specs=[idx_spec,
                  pl.BlockSpec((tb, 8, 128), lambda i: (i, 0, 0)),
                  pl.BlockSpec((tb, P_PAIRS), lambda i: (i, 0)),
                  tab_spec],
        out_specs=pl.BlockSpec((tb, P_PAIRS), lambda i: (i, 0)),
        out_shape=jax.ShapeDtypeStruct((t, P_PAIRS), F32),
        scratch_shapes=stream_scratch + [pltpu.VMEM((tb, 4 * P_PAIRS), F32), pltpu.VMEM((tb, 4 * P_PAIRS), F32)],
        compiler_params=params,
        name="peer_act",
    )(idx_chunks, h3, gate, u_pack)
    o3 = pl.pallas_call(
        _peer_out_kernel,
        grid=grid,
        in_specs=[idx_spec,
                  pl.BlockSpec((tb, P_PAIRS), lambda i: (i, 0)),
                  tab_spec],
        out_specs=pl.BlockSpec((tb, 8, 128), lambda i: (i, 0, 0)),
        out_shape=jax.ShapeDtypeStruct((t, 8, 128), F32),
        scratch_shapes=stream_scratch + [pltpu.VMEM((tb, 8 * P_PAIRS), F32), pltpu.VMEM((tb, 8 * P_PAIRS), F32)],
        compiler_params=params,
        name="peer_out",
    )(idx_chunks, w, v_pack)
    return o3.reshape(t, 1024)


def _resid_kernel(x_ref, p_ref, m_ref, nf_ref, o_ref, *, final):
    y = x_ref[0] + m_ref[0, 5:6, :] * p_ref[0]
    if final:
        y = y * lax.rsqrt(jnp.mean(y * y, axis=-1, keepdims=True) + EPS) * nf_ref[...]
    o_ref[0] = y


def _peer_layer(x3, norm2, mods, mod_index, w_q, keys, u_tab, v_tab, norm_f, final):
    b, l, d = x3.shape
    q, h = _norm_mod_proj(x3, norm2, mods, mod_index, 3, w_q, highest=True, emit_h=True, name="peer_query")
    idx_t, gate_t = _peer_topk(q.reshape(b * l, d), keys)
    y = _peer_experts(h.reshape(b * l, d), idx_t.T, gate_t.T, _pack_table(u_tab), _pack_table(v_tab))
    tok = pl.BlockSpec((1, TM, d), lambda bi, i: (bi, i, 0))
    return pl.pallas_call(
        functools.partial(_resid_kernel, final=final),
        grid=(b, l // TM),
        in_specs=[tok, tok, pl.BlockSpec((1, 6, d), lambda bi, i: (mod_index(bi, i), 0, 0)), _full((1, d))],
        out_specs=tok,
        out_shape=jax.ShapeDtypeStruct((b, l, d), F32),
        compiler_params=_params("arbitrary", "arbitrary"), name="peer_resid",
    )(x3, y.reshape(b, l, d), mods, norm_f.reshape(1, d))


def kernel(x, c, ctx, c_ctx, w_mod, b_mod, norm1, norm2, ev_w_in, ev_a_conv, ev_qkv_conv, ev_a_log, ev_dt_bias, ev_o_norm, ev_w_out, od_w_in, od_q_norm, od_kv_norm, od_w_uq, od_w_ukv, od_w_out, p_w_q, p_keys, p_u, p_v, norm_f):
    b, s, d = x.shape
    assert ctx.shape[1] == TM and s % TM == 0
    c_rows = jnp.concatenate([c, c_ctx[None, :], jnp.zeros((16 - b - 1, d), F32)], axis=0)
    seq = jnp.concatenate([ctx, x], axis=1)
    joint = lambda bi, i: jnp.where(i == 0, b, bi)
    latent = lambda bi, i: bi

    mods = _modulation(c_rows, w_mod[0], b_mod[0])
    w_in = ev_w_in[0]
    proj, ab_t = _norm_mod_proj(seq, norm1[0], mods, joint, 0, w_in[:, :3584].astype(BF16),
                                w_t=w_in[:, 3584:].T, name="even_in")
    y_a, qkv = _even_prep(proj, ev_a_conv[0], ev_qkv_conv[0])
    o_f, o_b = _gdn(qkv, ab_t, ev_a_log[0], ev_dt_bias[0])
    seq = _even_out(o_f, o_b, proj, y_a, seq, mods, joint, ev_o_norm[0], ev_w_out[0])
    seq = _peer_layer(seq, norm2[0], mods, joint, p_w_q[0], p_keys[0], p_u[0], p_v[0], norm_f, False)

    mods = _modulation(c_rows, w_mod[1], b_mod[1])
    w_in = od_w_in[0]
    k_rope = w_in[:, 640:704]
    w_ext = jnp.concatenate([w_in[:, :640], _pad_lanes(k_rope), _pad_lanes(_swap_cols(k_rope))], axis=1)
    (proj,) = _norm_mod_proj(seq, norm1[1], mods, joint, 0, w_ext.astype(BF16), name="mla_in")
    cos, sin = _rope_tables(s)
    qn, qr, kn, kr, v = _mla_expand(proj, od_q_norm[0], od_kv_norm[0], od_w_uq[0], od_w_ukv[0], cos, sin)
    o = _attention(qn, qr, kn, kr, v)
    xs = _attn_out(o, seq, mods, od_w_out[0])
    return _peer_layer(xs, norm2[1], mods, latent, p_w_q[1], p_keys[1], p_u[1], p_v[1], norm_f, True)
```

```python
import functools
import math

import jax
import jax.numpy as jnp
from jax import lax
from jax.experimental import pallas as pl
from jax.experimental.pallas import tpu as pltpu

F32 = jnp.float32
BF16 = jnp.bfloat16
HIGHEST = lax.Precision.HIGHEST

EPS = 1e-6
GRID_W = 64
ROPE_BASE = 10000.0
TM = 256
GDN_C = 128
GDN_HEADS = 4
C_HEADS = 8
C_NOPE = 128
C_ROPE = 64
P_HEADS = 8
P_NKEYS = 128
P_TOPK = 16
P_PAIRS = P_HEADS * P_TOPK
PEER_NT = 32
TOPK_TB = 256
ATTN_TQ = 256
VMEM_LIMIT = 48 * 1024 * 1024
PEER_VMEM = 50 * 1024 * 1024


def _nt_dot(a, b, precision=None):
    return lax.dot_general(a, b, (((1,), (1,)), ((), ())), precision=precision, preferred_element_type=F32)


def _tn_dot(a, b, precision=None):
    return lax.dot_general(a, b, (((0,), (0,)), ((), ())), precision=precision, preferred_element_type=F32)


def _dot(a, b, precision=None):
    return jnp.dot(a, b, precision=precision, preferred_element_type=F32)


_NN = (((1,), (0,)), ((), ()))
_NT = (((1,), (1,)), ((), ()))
_TN = (((0,), (0,)), ((), ()))


def _split2(x):
    hi = x.astype(BF16)
    return hi, (x - hi.astype(F32)).astype(BF16)


def _mm3(a, b, dims):
    f = lambda x, y: lax.dot_general(x, y, dims, preferred_element_type=F32)
    return f(a[0], b[0]) + (f(a[0], b[1]) + f(a[1], b[0]))


def _mm_exact(x, m, dims):
    hi = x.astype(BF16)
    r = x - hi.astype(F32)
    mid = r.astype(BF16)
    lo = (r - mid.astype(F32)).astype(BF16)
    f = lambda y: lax.dot_general(y, m, dims, preferred_element_type=F32)
    return f(hi) + (f(mid) + f(lo))


def _silu(x):
    return x * jax.nn.sigmoid(x)


def _params(*sem):
    return pltpu.CompilerParams(dimension_semantics=sem, vmem_limit_bytes=VMEM_LIMIT)


def _full(shape):
    return pl.BlockSpec(shape, lambda *_: (0,) * len(shape))


def _mod_kernel(c_ref, w_ref, b_ref, o_ref):
    o_ref[...] = _dot(_silu(c_ref[...]), w_ref[...], HIGHEST) + b_ref[...]


def _modulation(c_rows, w, b):
    r, d = c_rows.shape
    n = w.shape[1]
    tn = 512
    out = pl.pallas_call(
        _mod_kernel,
        grid=(n // tn,),
        in_specs=[_full((r, d)), pl.BlockSpec((d, tn), lambda j: (0, j)), pl.BlockSpec((1, tn), lambda j: (0, j))],
        out_specs=pl.BlockSpec((r, tn), lambda j: (0, j)),
        out_shape=jax.ShapeDtypeStruct((r, n), F32),
        compiler_params=_params("arbitrary"),
        name="modulation",
    )(c_rows, w, b.reshape(1, n))
    return out.reshape(r, 6, d)


def _proj_kernel(*refs, shift_row, highest, has_wt, emit_h):
    x_ref, g_ref, m_ref, w_ref = refs[:4]
    rest = list(refs[4:])
    wt_ref = rest.pop(0) if has_wt else None
    o_ref = rest.pop(0)
    x = x_ref[0]
    h = x * lax.rsqrt(jnp.mean(x * x, axis=-1, keepdims=True) + EPS) * g_ref[...]
    h = h * (1.0 + m_ref[0, shift_row + 1:shift_row + 2, :]) + m_ref[0, shift_row:shift_row + 1, :]
    if highest:
        o_ref[0] = _mm3(_split2(h), (w_ref[0], w_ref[1]), _NN).astype(o_ref.dtype)
    else:
        o_ref[0] = _dot(h.astype(BF16), w_ref[...]).astype(o_ref.dtype)
    if has_wt:
        rest.pop(0)[0] = _nt_dot(wt_ref[...], h, HIGHEST)
        edge_ref = rest.pop(0)
        edge_ref[0, 0, 0:1, :] = o_ref[0, 0:1, :]
        edge_ref[0, 0, 1:2, :] = o_ref[0, TM - 1:TM, :]
    if emit_h:
        rest.pop(0)[0] = h


def _norm_mod_proj(x3, gain, mods, mod_index, shift_row, w, *, highest=False, w_t=None, emit_h=False, name):
    b, l, d = x3.shape
    n = w.shape[1]
    if highest:
        w = jnp.stack(_split2(w))
    grid = (b, l // TM)
    in_specs = [pl.BlockSpec((1, TM, d), lambda bi, i: (bi, i, 0)),
                _full((1, d)),
                pl.BlockSpec((1, 6, d), lambda bi, i: (mod_index(bi, i), 0, 0)),
                _full(w.shape)]
    args = [x3, gain.reshape(1, d), mods, w]
    out_specs = [pl.BlockSpec((1, TM, n), lambda bi, i: (bi, i, 0))]
    out_shape = [jax.ShapeDtypeStruct((b, l, n), F32)]
    if w_t is not None:
        in_specs.append(_full(w_t.shape))
        args.append(w_t)
        out_specs.append(pl.BlockSpec((1, w_t.shape[0], TM), lambda bi, i: (bi, 0, i)))
        out_shape.append(jax.ShapeDtypeStruct((b, w_t.shape[0], l), F32))
        out_specs.append(pl.BlockSpec((1, 1, 2, n), lambda bi, i: (bi, i, 0, 0)))
        out_shape.append(jax.ShapeDtypeStruct((b, l // TM, 2, n), F32))
    if emit_h:
        out_specs.append(pl.BlockSpec((1, TM, d), lambda bi, i: (bi, i, 0)))
        out_shape.append(jax.ShapeDtypeStruct((b, l, d), F32))
    return pl.pallas_call(
        functools.partial(_proj_kernel, shift_row=shift_row, highest=highest, has_wt=w_t is not None, emit_h=emit_h),
        grid=grid, in_specs=in_specs, out_specs=out_specs, out_shape=out_shape,
        compiler_params=_params("arbitrary", "arbitrary"), name=name,
    )(*args)


def _even_prep_kernel(p_ref, halo_ref, aconv_ref, qconv_ref, ya_ref, qkv_ref):
    tm = p_ref.shape[1]
    rows = lax.broadcasted_iota(jnp.int32, (tm, 512), 0)
    first = rows == 0
    last = rows == tm - 1

    def conv(x, x_prev, x_next, w):
        xm = jnp.where(first, x_prev, pltpu.roll(x, 1, 0))
        xp = jnp.where(last, x_next, pltpu.roll(x, tm - 1, 0))
        return xm * w[0:1] + x * w[1:2] + xp * w[2:3]

    def cols(k):
        return p_ref[0, :, 512 * k:512 * (k + 1)]

    def halo(r, k):
        return halo_ref[0, 0, r:r + 1, 512 * k:512 * (k + 1)]

    ca = conv(cols(1) * cols(2), halo(0, 1) * halo(0, 2), halo(1, 1) * halo(1, 2), aconv_ref[...])
    ya_ref[0] = cols(0) * ca
    for s in range(3):
        y = _silu(conv(cols(3 + s), halo(0, 3 + s), halo(1, 3 + s), qconv_ref[:, 512 * s:512 * (s + 1)]))
        if s < 2:
            parts = []
            for h in range(GDN_HEADS):
                t = y[:, 128 * h:128 * (h + 1)]
                parts.append(t * lax.rsqrt(jnp.sum(t * t, axis=-1, keepdims=True) + EPS))
            y = jnp.concatenate(parts, axis=-1)
        qkv_ref[0, :, 512 * s:512 * (s + 1)] = y


def _even_prep(proj, edges, a_conv, qkv_conv):
    b, l, n = proj.shape
    nb = l // TM
    firstrows = edges[:, :, 0, :]
    lastrows = edges[:, :, 1, :]
    blk = jnp.arange(nb)
    prev = jnp.where((blk >= 2)[None, :, None], jnp.roll(lastrows, 1, axis=1), 0.0)
    nxt = jnp.where(((blk >= 1) & (blk < nb - 1))[None, :, None], jnp.roll(firstrows, -1, axis=1), 0.0)
    halo = jnp.stack([prev, nxt], axis=2)
    return pl.pallas_call(
        _even_prep_kernel,
        grid=(b, nb),
        in_specs=[pl.BlockSpec((1, TM, 3072), lambda bi, i: (bi, i, 0)),
                  pl.BlockSpec((1, 1, 2, n), lambda bi, i: (bi, i, 0, 0)),
                  _full(a_conv.shape), _full(qkv_conv.shape)],
        out_specs=[pl.BlockSpec((1, TM, 512), lambda bi, i: (bi, i, 0)),
                   pl.BlockSpec((1, TM, 1536), lambda bi, i: (bi, i, 0))],
        out_shape=[jax.ShapeDtypeStruct((b, l, 512), F32), jax.ShapeDtypeStruct((b, l, 1536), F32)],
        compiler_params=_params("arbitrary", "arbitrary"), name="even_prep",
    )(proj, halo, a_conv, qkv_conv)


def _gdn_kernel(qf_ref, qb_ref, abf_ref, abb_ref, alog_ref, dtb_ref, of_ref, ob_ref, s_ref):
    n = pl.program_id(1)
    c = GDN_C

    @pl.when(n == 0)
    def _():
        s_ref[...] = jnp.zeros_like(s_ref)

    row = lax.broadcasted_iota(jnp.int32, (c, c), 0)
    col = lax.broadcasted_iota(jnp.int32, (c, c), 1)
    eye = row == col
    eyef = jnp.where(eye, 1.0, 0.0)
    same = {b: (row // b) == (col // b) for b in (8, 16, 32, 64)}
    same[2 * 64] = row >= 0
    merge = {b: same[2 * b] & jnp.logical_not(same[b]) for b in (8, 16, 32, 64)}
    ones = jnp.ones((c, c), BF16)
    scale = 128 ** -0.5
    for d, (x_ref, ab_ref, o_ref) in enumerate(((qf_ref, abf_ref, of_ref), (qb_ref, abb_ref, ob_ref))):
        tri = (row >= col) if d == 0 else (row <= col)
        strict = (row > col) if d == 0 else (row < col)
        trif = jnp.where(tri, 1.0, 0.0).astype(BF16)
        ab = ab_ref[0]
        g4 = -jnp.exp(alog_ref[4 * d:4 * d + 4, :]) * jax.nn.softplus(ab[4 * d:4 * d + 4, :] + dtb_ref[4 * d:4 * d + 4, :])
        beta4 = jax.nn.sigmoid(ab[8 + 4 * d:12 + 4 * d, :])
        heads = range(GDN_HEADS)
        gt = [jnp.broadcast_to(g4[h:h + 1, :], (c, c)) for h in heads]
        bt = [jnp.broadcast_to(beta4[h:h + 1, :], (c, c)) for h in heads]
        dcol = [_mm_exact(jnp.where(tri, gt[h], 0.0), ones, _NN) for h in heads]
        drow = [_mm_exact(gt[h], trif, _NT) for h in heads]
        tot = [_mm_exact(gt[h], ones, _NN) for h in heads]
        bc = [_mm_exact(jnp.where(eye, bt[h], 0.0), ones, _NN) for h in heads]
        gamma = [jnp.exp(jnp.where(tri, dcol[h] - drow[h], -jnp.inf)) for h in heads]
        edec = [jnp.exp(dcol[h]) for h in heads]
        q = [x_ref[0, :, 128 * h:128 * (h + 1)] * scale for h in heads]
        k = [x_ref[0, :, 512 + 128 * h:512 + 128 * (h + 1)] for h in heads]
        v = [x_ref[0, :, 1024 + 128 * h:1024 + 128 * (h + 1)] for h in heads]
        kb = [k[h] * bc[h] for h in heads]
        k2 = [_split2(k[h]) for h in heads]
        a = [jnp.where(strict, _mm3(_split2(kb[h]), k2[h], _NT) * gamma[h], 0.0) for h in heads]
        d8 = [jnp.where(same[8], a[h], 0.0) for h in heads]
        t = [eyef - d8[h] for h in heads]
        p2 = [_split2(d8[h]) for h in heads]
        for _ in range(2):
            p2 = [_split2(_mm3(p2[h], p2[h], _NN)) for h in heads]
            t = [t[h] + _mm3(_split2(t[h]), p2[h], _NN) for h in heads]
        for blk in (8, 16, 32, 64):
            off = [jnp.where(merge[blk], a[h], 0.0) for h in heads]
            t2 = [_split2(t[h]) for h in heads]
            lt = [_mm3(_split2(off[h]), t2[h], _NN) for h in heads]
            t = [t[h] - _mm3(t2[h], _split2(lt[h]), _NN) for h in heads]
        t2 = [_split2(t[h]) for h in heads]
        u = [_mm3(t2[h], _split2(v[h] * bc[h]), _NN) for h in heads]
        w = [_mm3(t2[h], _split2(kb[h] * edec[h]), _NN) for h in heads]
        intra = [jnp.where(tri, _mm3(_split2(q[h]), k2[h], _NT) * gamma[h], 0.0) for h in heads]
        s = [s_ref[4 * d + h] for h in heads]
        s2 = [_split2(s[h]) for h in heads]
        v_new = [u[h] - _mm3(_split2(w[h]), s2[h], _NN) for h in heads]
        v2 = [_split2(v_new[h]) for h in heads]
        for h in heads:
            o_ref[0, :, 128 * h:128 * (h + 1)] = (_mm3(_split2(q[h] * edec[h]), s2[h], _NN)
                                                  + _mm3(_split2(intra[h]), v2[h], _NN))
            s_ref[4 * d + h] = s[h] * jnp.exp(tot[h]) + _mm3(_split2(k[h] * jnp.exp(tot[h] - dcol[h])), v2[h], _TN)


def _gdn(qkv, ab_t, a_log, dt_bias):
    b, l, _ = qkv.shape
    nb = l // GDN_C
    nctx = TM // GDN_C

    def bwd(n):
        return jnp.where(n < nctx, nctx - 1 - n, nb + nctx - 1 - n)

    alog = jnp.broadcast_to(a_log.reshape(8, 1), (8, 128))
    dtb = jnp.broadcast_to(dt_bias.reshape(8, 1), (8, 128))
    return pl.pallas_call(
        _gdn_kernel,
        grid=(b, nb),
        in_specs=[pl.BlockSpec((1, GDN_C, 1536), lambda bi, n: (bi, n, 0)),
                  pl.BlockSpec((1, GDN_C, 1536), lambda bi, n: (bi, bwd(n), 0)),
                  pl.BlockSpec((1, 16, GDN_C), lambda bi, n: (bi, 0, n)),
                  pl.BlockSpec((1, 16, GDN_C), lambda bi, n: (bi, 0, bwd(n))),
                  _full((8, 128)), _full((8, 128))],
        out_specs=[pl.BlockSpec((1, GDN_C, 512), lambda bi, n: (bi, n, 0)),
                   pl.BlockSpec((1, GDN_C, 512), lambda bi, n: (bi, bwd(n), 0))],
        out_shape=[jax.ShapeDtypeStruct((b, l, 512), F32), jax.ShapeDtypeStruct((b, l, 512), F32)],
        scratch_shapes=[pltpu.VMEM((2 * GDN_HEADS, 128, 128), F32)],
        compiler_params=_params("arbitrary", "arbitrary"), name="gdn",
    )(qkv, qkv, ab_t, ab_t, alog, dtb)


def _even_out_kernel(of_ref, ob_ref, z_ref, ya_ref, x_ref, m_ref, on_ref, w_ref, o_ref):
    o = of_ref[0] + ob_ref[0]
    z = z_ref[0]
    parts = []
    for h in range(GDN_HEADS):
        oh = o[:, 128 * h:128 * (h + 1)]
        y = oh * lax.rsqrt(jnp.mean(oh * oh, axis=-1, keepdims=True) + EPS) * on_ref[...]
        parts.append(y * _silu(z[:, 128 * h:128 * (h + 1)]))
    yb = jnp.concatenate(parts, axis=-1)
    y = _dot(ya_ref[0].astype(BF16), w_ref[0:512, :]) + _dot(yb.astype(BF16), w_ref[512:1536, :])
    o_ref[0] = x_ref[0] + m_ref[0, 2:3, :] * y


def _even_out(o_f, o_b, proj, y_a, x3, mods, mod_index, o_norm, w_out):
    b, l, d = x3.shape
    tok = lambda n: pl.BlockSpec((1, TM, n), lambda bi, i: (bi, i, 0))
    return pl.pallas_call(
        _even_out_kernel,
        grid=(b, l // TM),
        in_specs=[tok(512), tok(512),
                  pl.BlockSpec((1, TM, 512), lambda bi, i: (bi, i, 6)),
                  tok(512), tok(d),
                  pl.BlockSpec((1, 6, d), lambda bi, i: (mod_index(bi, i), 0, 0)),
                  _full((1, 128)), _full(w_out.shape)],
        out_specs=tok(d),
        out_shape=jax.ShapeDtypeStruct((b, l, d), F32),
        compiler_params=_params("arbitrary", "arbitrary"), name="even_out",
    )(o_f, o_b, proj, y_a, x3, mods, o_norm.reshape(1, 128), w_out.astype(BF16))


def _rope_tables(s):
    t = jnp.arange(s, dtype=jnp.int32)
    n = C_ROPE // 4
    freqs = ROPE_BASE ** (-jnp.arange(n, dtype=F32) / n)

    def seg(pos):
        ang = pos.astype(F32)[:, None] * freqs[None, :]
        return (jnp.concatenate([jnp.cos(ang), jnp.cos(ang)], -1), jnp.concatenate([-jnp.sin(ang), jnp.sin(ang)], -1))

    cr, sr = seg(t // GRID_W)
    cc, sc = seg(t % GRID_W)
    cos = jnp.concatenate([jnp.ones((TM, C_ROPE), F32), jnp.concatenate([cr, cc], -1)], 0)
    sin = jnp.concatenate([jnp.zeros((TM, C_ROPE), F32), jnp.concatenate([sr, sc], -1)], 0)
    return cos, sin


def _swap_cols(w):
    n = C_ROPE // 4
    perm = jnp.concatenate([jnp.arange(n) + n, jnp.arange(n), jnp.arange(n) + 3 * n, jnp.arange(n) + 2 * n])
    return w[..., perm]


def _pad_lanes(w, width=128):
    return jnp.concatenate([w, jnp.zeros(w.shape[:-1] + (width - w.shape[-1],), w.dtype)], axis=-1)


def _mla_expand_kernel(p_ref, qn_ref, kvn_ref, cos_ref, sin_ref, wq_ref, wqs_ref, wk_ref, wv_ref,
                       oq_ref, ok_ref, ov_ref):
    def norm(x, g):
        return (x * lax.rsqrt(jnp.mean(x * x, axis=-1, keepdims=True) + EPS) * g).astype(BF16)

    cq = norm(p_ref[0, :, 0:384], qn_ref[...])
    ckv = norm(p_ref[0, :, 384:640], kvn_ref[...])
    cos = cos_ref[...]
    sin = sin_ref[...]
    scale = (C_NOPE + C_ROPE) ** -0.5
    cos_h = jnp.concatenate([jnp.ones_like(cos), cos] * C_HEADS, axis=-1)
    sin_h = jnp.concatenate([jnp.zeros_like(sin), sin] * C_HEADS, axis=-1)
    oq_ref[0] = ((_dot(cq, wq_ref[...]) * cos_h + _dot(cq, wqs_ref[...]) * sin_h) * scale).astype(BF16)
    kn = _dot(ckv, wk_ref[...]).astype(BF16)
    kr = (p_ref[0, :, 640:768] * cos + p_ref[0, :, 768:896] * sin).astype(BF16)
    parts = []
    for h in range(C_HEADS):
        parts += [kn[:, 128 * h:128 * (h + 1)], kr]
    ok_ref[0] = jnp.concatenate(parts, axis=-1)
    ov_ref[0] = _dot(ckv, wv_ref[...]).astype(BF16)


def _mla_expand(proj, q_norm, kv_norm, w_uq, w_ukv, cos, sin):
    b, l, n = proj.shape
    hd = C_HEADS * 128
    wq = w_uq.reshape(-1, C_HEADS, C_NOPE + C_ROPE)
    rope = wq[:, :, C_NOPE:]
    w_q = jnp.concatenate([wq[:, :, :C_NOPE], _pad_lanes(rope)], axis=-1).reshape(-1, 2 * hd).astype(BF16)
    w_qs = jnp.concatenate([jnp.zeros_like(wq[:, :, :C_NOPE]), _pad_lanes(_swap_cols(rope))],
                           axis=-1).reshape(-1, 2 * hd).astype(BF16)
    wkv = w_ukv.reshape(-1, C_HEADS, 2 * 128)
    wk = wkv[:, :, :128].reshape(-1, hd).astype(BF16)
    wv = wkv[:, :, 128:].reshape(-1, hd).astype(BF16)
    tok = lambda m: pl.BlockSpec((1, TM, m), lambda bi, i: (bi, i, 0))
    tab = pl.BlockSpec((TM, 128), lambda bi, i: (i, 0))
    return pl.pallas_call(
        _mla_expand_kernel,
        grid=(b, l // TM),
        in_specs=[tok(n), _full((1, 384)), _full((1, 256)), tab, tab,
                  _full(w_q.shape), _full(w_qs.shape), _full(wk.shape), _full(wv.shape)],
        out_specs=[tok(2 * hd), tok(2 * hd), tok(hd)],
        out_shape=[jax.ShapeDtypeStruct((b, l, 2 * hd), BF16), jax.ShapeDtypeStruct((b, l, 2 * hd), BF16),
                   jax.ShapeDtypeStruct((b, l, hd), BF16)],
        compiler_params=_params("arbitrary", "arbitrary"), name="mla_expand",
    )(proj, q_norm.reshape(1, -1), kv_norm.reshape(1, -1), _pad_lanes(cos), _pad_lanes(sin), w_q, w_qs, wk, wv)


def _attn_kernel(q_ref, k_ref, v_ref, o_ref):
    s = _nt_dot(q_ref[0], k_ref[0])
    m = jnp.max(s, axis=-1, keepdims=True)
    p = jnp.exp(s - m)
    den = jnp.sum(p, axis=-1, keepdims=True)
    o_ref[0] = (_dot(p.astype(BF16), v_ref[0]) / den).astype(o_ref.dtype)


def _attention(q, k, v):
    b, l, hd = v.shape
    s = l - TM
    return pl.pallas_call(
        _attn_kernel,
        grid=(b, C_HEADS, s // ATTN_TQ),
        in_specs=[pl.BlockSpec((1, ATTN_TQ, 256), lambda bi, h, i: (bi, i + TM // ATTN_TQ, h)),
                  pl.BlockSpec((1, l, 256), lambda bi, h, i: (bi, 0, h)),
                  pl.BlockSpec((1, l, 128), lambda bi, h, i: (bi, 0, h))],
        out_specs=pl.BlockSpec((1, ATTN_TQ, 128), lambda bi, h, i: (bi, i, h)),
        out_shape=jax.ShapeDtypeStruct((b, s, hd), BF16),
        compiler_params=_params("arbitrary", "arbitrary", "arbitrary"), name="attention",
    )(q, k, v)


def _attn_out_kernel(o_ref, x_ref, m_ref, w_ref, out_ref):
    out_ref[0] = x_ref[0] + m_ref[0, 2:3, :] * _dot(o_ref[0], w_ref[...])


def _attn_out(o, x3, mods, w_out):
    b, s, d = o.shape[0], o.shape[1], x3.shape[2]
    return pl.pallas_call(
        _attn_out_kernel,
        grid=(b, s // TM),
        in_specs=[pl.BlockSpec((1, TM, o.shape[2]), lambda bi, i: (bi, i, 0)),
                  pl.BlockSpec((1, TM, d), lambda bi, i: (bi, i + 1, 0)),
                  pl.BlockSpec((1, 6, d), lambda bi, i: (bi, 0, 0)),
                  _full(w_out.shape)],
        out_specs=pl.BlockSpec((1, TM, d), lambda bi, i: (bi, i, 0)),
        out_shape=jax.ShapeDtypeStruct((b, s, d), F32),
        compiler_params=_params("arbitrary", "arbitrary"), name="attn_out",
    )(o, x3, mods, w_out.astype(BF16))


def _topk_rows(s, k, ids=None):
    if ids is None:
        ids = lax.broadcasted_iota(jnp.int32, s.shape, 0)
    big = jnp.iinfo(jnp.int32).max
    vals, idxs = [], []
    for _ in range(k):
        m = jnp.max(s, axis=0, keepdims=True)
        am = jnp.min(jnp.where(s == m, ids, big), axis=0, keepdims=True)
        vals.append(m)
        idxs.append(am)
        s = jnp.where(ids == am, -jnp.inf, s)
    return vals, idxs


def _candidates(v1, v2):
    tl = v1[0].shape[1]
    v2s = jnp.concatenate(v2, axis=0)
    sub8 = lax.broadcasted_iota(jnp.int32, (8, tl), 0)
    sub16 = lax.broadcasted_iota(jnp.int32, (P_TOPK, tl), 0)
    vals = [v1[0] + v2s]
    ids = [sub16]
    for a in range(1, 5):
        nb = P_TOPK // (a + 1)
        vals.append(jnp.where(sub8 < nb, v1[a] + v2s[:8], -jnp.inf))
        ids.append(sub8 + P_TOPK * a)
    rest = [(a, b) for a in range(5, P_TOPK) for b in range(P_TOPK // (a + 1))]
    for group in (rest[:8], rest[8:]):
        rows = [v1[a] + v2[b] for a, b in group]
        rid = [jnp.full((1, tl), P_TOPK * a + b, jnp.int32) for a, b in group]
        pad = 8 - len(group)
        if pad:
            rows.append(jnp.full((pad, tl), -jnp.inf, F32))
            rid.append(jnp.full((pad, tl), P_TOPK * P_TOPK, jnp.int32))
        vals.append(jnp.concatenate(rows, axis=0))
        ids.append(jnp.concatenate(rid, axis=0))
    return jnp.concatenate(vals, axis=0), jnp.concatenate(ids, axis=0)


def _peer_topk_kernel(q_ref, key_ref, idx_ref, gate_ref):
    tl = 128
    for sub in range(q_ref.shape[0] // tl):
        for h in range(P_HEADS):
            q = q_ref[sub * tl:(sub + 1) * tl, 128 * h:128 * (h + 1)]
            st = _nt_dot(key_ref[h], q, HIGHEST)
            v1, i1 = _topk_rows(st[:P_NKEYS], P_TOPK)
            v2, i2 = _topk_rows(st[P_NKEYS:], P_TOPK)
            cand, cand_pos = _candidates(v1, v2)
            sc, pos = _topk_rows(cand, P_TOPK, cand_pos)
            sc = jnp.concatenate(sc, axis=0)
            pos = jnp.concatenate(pos, axis=0)
            pa = pos >> 4
            pb = pos & (P_TOPK - 1)
            e1 = jnp.zeros_like(pos)
            e2 = jnp.zeros_like(pos)
            for r in range(P_TOPK):
                e1 = jnp.where(pa == r, i1[r], e1)
                e2 = jnp.where(pb == r, i2[r], e2)
            ex = jnp.exp(sc - sc[0:1])
            idx_ref[P_TOPK * h:P_TOPK * (h + 1), sub * tl:(sub + 1) * tl] = (e1 * P_NKEYS + e2) * 4
            gate_ref[P_TOPK * h:P_TOPK * (h + 1), sub * tl:(sub + 1) * tl] = ex / jnp.sum(ex, axis=0, keepdims=True)


def _peer_topk(q, keys):
    t = q.shape[0]
    z = jnp.zeros((P_HEADS, P_NKEYS, 64), F32)
    kbd = jnp.concatenate([jnp.concatenate([keys[:, 0], z], -1), jnp.concatenate([z, keys[:, 1]], -1)], axis=1)
    return pl.pallas_call(
        _peer_topk_kernel,
        grid=(t // TOPK_TB,),
        in_specs=[pl.BlockSpec((TOPK_TB, 1024), lambda i: (i, 0)), _full(kbd.shape)],
        out_specs=[pl.BlockSpec((P_PAIRS, TOPK_TB), lambda i: (0, i)), pl.BlockSpec((P_PAIRS, TOPK_TB), lambda i: (0, i))],
        out_shape=[jax.ShapeDtypeStruct((P_PAIRS, t), jnp.int32), jax.ShapeDtypeStruct((P_PAIRS, t), F32)],
        compiler_params=_params("arbitrary"), name="peer_topk",
    )(q, kbd)


def _pack_table(tab):
    e = tab.shape[0]
    tb = lax.bitcast_convert_type(tab.astype(BF16), jnp.uint16).astype(jnp.uint32).reshape(e * 4, 2, 128)
    word = tb[:, 0, :] | (tb[:, 1, :] << 16)
    return lax.bitcast_convert_type(word, jnp.int32)


def _gather_rows(idx_smem, tab_ref, tt, g_ref):
    for j in range(P_PAIRS):
        row = idx_smem[tt * P_PAIRS + j]
        g_ref[pl.ds(4 * j, 4), :] = tab_ref[pl.ds(pl.multiple_of(row, 4), 4), :]


def _index_stream(idx_hbm, bufs, sems, per_token):
    i = pl.program_id(0)
    n = pl.num_programs(0)

    def chunk(c, k):
        return pltpu.make_async_copy(idx_hbm.at[c], bufs[k], sems.at[k])

    @pl.when(i == 0)
    def _():
        chunk(0, 0).start()

    chunk(2 * i + 1, 1).start()
    chunk(2 * i, 0).wait()
    for tt in range(PEER_NT):
        per_token(tt, tt, bufs[0])

    @pl.when(i + 1 < n)
    def _():
        chunk(2 * i + 2, 0).start()

    chunk(2 * i + 1, 1).wait()
    for tt in range(PEER_NT):
        per_token(PEER_NT + tt, tt, bufs[1])


def _peer_act_kernel(idx_hbm, h_ref, gate_ref, tab_ref, w_ref, ia, ib, sems, g0, g1, rlo_ref, rhi_ref):
    half = 4 * P_PAIRS
    lane = lax.broadcasted_iota(jnp.int32, (16, half), 1)
    sub = lax.broadcasted_iota(jnp.int32, (16, half), 0)
    diag = (lane & 7) == (sub & 7)
    zero = jnp.zeros((8, 128), BF16)
    gbufs = (g0, g1)

    def per_token(t, tt, idx_smem):
        g_ref = gbufs[tt % 2]
        _gather_rows(idx_smem, tab_ref, tt, g_ref)
        gb = pltpu.bitcast(g_ref[...], BF16)
        g2 = jnp.concatenate([gb[:half], gb[half:]], axis=1)
        hb = jnp.concatenate([h_ref[t:t + 1, 128 * s:128 * (s + 1)] for s in range(8)], axis=0).astype(BF16)
        lhs = jnp.concatenate([jnp.concatenate([hb, zero], axis=1), jnp.concatenate([zero, hb], axis=1)], axis=0)
        r = jnp.where(diag, _nt_dot(lhs, g2), 0.0)
        rlo_ref[t:t + 1, :] = jnp.sum(r[:8], axis=0, keepdims=True)
        rhi_ref[t:t + 1, :] = jnp.sum(r[8:], axis=0, keepdims=True)

    _index_stream(idx_hbm, (ia, ib), sems, per_token)

    col = lax.broadcasted_iota(jnp.int32, (half, P_PAIRS), 0)
    pair = lax.broadcasted_iota(jnp.int32, (half, P_PAIRS), 1)

    def fold(r, first_pair):
        f = jnp.where((col >> 3) + first_pair == pair, 1.0, 0.0).astype(BF16)
        hi = r.astype(BF16)
        lo = (r - hi.astype(F32)).astype(BF16)
        return _dot(hi, f) + _dot(lo, f)

    act = fold(rlo_ref[...], 0) + fold(rhi_ref[...], P_PAIRS // 2)
    gelu = 0.5 * act * (1.0 + lax.erf(act * (1.0 / math.sqrt(2.0))))
    w_ref[...] = gate_ref[...] * gelu


def _peer_out_kernel(idx_hbm, w_ref, tab_ref, x_ref, m_ref, nf_ref, o_ref, ia, ib, sems, g0, g1,
                     whi_ref, wlo_ref, y_ref, *, final):
    lane = lax.broadcasted_iota(jnp.int32, (8, 8 * P_PAIRS), 1)
    sub = lax.broadcasted_iota(jnp.int32, (8, 8 * P_PAIRS), 0)
    diag = (lane & 7) == sub
    pair = lax.broadcasted_iota(jnp.int32, (P_PAIRS, 8 * P_PAIRS), 0)
    col = lax.broadcasted_iota(jnp.int32, (P_PAIRS, 8 * P_PAIRS), 1)
    spread = jnp.where((col >> 3) == pair, 1.0, 0.0).astype(BF16)
    w = w_ref[...]
    w_hi = w.astype(BF16)
    w_lo = (w - w_hi.astype(F32)).astype(BF16)
    whi_ref[...] = _dot(w_hi, spread)
    wlo_ref[...] = _dot(w_lo, spread)
    gbufs = (g0, g1)

    def per_token(t, tt, idx_smem):
        g_ref = gbufs[tt % 2]
        _gather_rows(idx_smem, tab_ref, tt, g_ref)
        gb = pltpu.bitcast(g_ref[...], BF16)
        rows = [jnp.where(diag, jnp.broadcast_to(r[t:t + 1, :], (8, 8 * P_PAIRS)), 0.0).astype(BF16)
                for r in (whi_ref, wlo_ref)]
        o = _dot(jnp.concatenate(rows, axis=0), gb)
        o = o[:8] + o[8:]
        for s in range(8):
            y_ref[t:t + 1, 128 * s:128 * (s + 1)] = o[s:s + 1, :]

    _index_stream(idx_hbm, (ia, ib), sems, per_token)

    y = x_ref[...] + m_ref[0, 5:6, :] * y_ref[...]
    if final:
        y = y * lax.rsqrt(jnp.mean(y * y, axis=-1, keepdims=True) + EPS) * nf_ref[...]
    o_ref[...] = y


def _peer_experts(x, h, idx4, gate, u_pack, v_pack, mods, mod_of_block, norm_f, final):
    t, d = h.shape
    tb = 2 * PEER_NT
    idx_chunks = idx4.reshape(t // PEER_NT, PEER_NT * P_PAIRS)
    grid = (t // tb,)
    idx_spec = pl.BlockSpec(memory_space=pl.ANY)
    tab_spec = pl.BlockSpec(memory_space=pltpu.VMEM)
    params = pltpu.CompilerParams(dimension_semantics=("arbitrary",), vmem_limit_bytes=PEER_VMEM)
    stream_scratch = [pltpu.SMEM((PEER_NT * P_PAIRS,), jnp.int32), pltpu.SMEM((PEER_NT * P_PAIRS,), jnp.int32),
                      pltpu.SemaphoreType.DMA((2,)),
                      pltpu.VMEM((4 * P_PAIRS, 128), jnp.int32), pltpu.VMEM((4 * P_PAIRS, 128), jnp.int32)]
    w = pl.pallas_call(
        _peer_act_kernel,
        grid=grid,
        in_specs=[idx_spec,
                  pl.BlockSpec((tb, d), lambda i: (i, 0)),
                  pl.BlockSpec((tb, P_PAIRS), lambda i: (i, 0)),
                  tab_spec],
        out_specs=pl.BlockSpec((tb, P_PAIRS), lambda i: (i, 0)),
        out_shape=jax.ShapeDtypeStruct((t, P_PAIRS), F32),
        scratch_shapes=stream_scratch + [pltpu.VMEM((tb, 4 * P_PAIRS), F32), pltpu.VMEM((tb, 4 * P_PAIRS), F32)],
        compiler_params=params,
        name="peer_act",
    )(idx_chunks, h, gate, u_pack)
    return pl.pallas_call(
        functools.partial(_peer_out_kernel, final=final),
        grid=grid,
        in_specs=[idx_spec,
                  pl.BlockSpec((tb, P_PAIRS), lambda i: (i, 0)),
                  tab_spec,
                  pl.BlockSpec((tb, d), lambda i: (i, 0)),
                  pl.BlockSpec((1, 6, d), lambda i: (mod_of_block(i), 0, 0)),
                  _full((1, d))],
        out_specs=pl.BlockSpec((tb, d), lambda i: (i, 0)),
        out_shape=jax.ShapeDtypeStruct((t, d), F32),
        scratch_shapes=stream_scratch + [pltpu.VMEM((tb, 8 * P_PAIRS), F32), pltpu.VMEM((tb, 8 * P_PAIRS), F32),
                                         pltpu.VMEM((tb, d), F32)],
        compiler_params=params,
        name="peer_out",
    )(idx_chunks, w, v_pack, x, mods, norm_f.reshape(1, d))


def _peer_layer(x3, norm2, mods, mod_index, w_q, keys, u_tab, v_tab, norm_f, final):
    b, l, d = x3.shape
    q, h = _norm_mod_proj(x3, norm2, mods, mod_index, 3, w_q, highest=True, emit_h=True, name="peer_query")
    idx_t, gate_t = _peer_topk(q.reshape(b * l, d), keys)
    per_seq = l // (2 * PEER_NT)
    mod_of_block = lambda i: mod_index(i // per_seq, (i % per_seq) // (TM // (2 * PEER_NT)))
    y = _peer_experts(x3.reshape(b * l, d), h.reshape(b * l, d), idx_t.T, gate_t.T,
                      _pack_table(u_tab), _pack_table(v_tab), mods, mod_of_block, norm_f, final)
    return y.reshape(b, l, d)


def kernel(x, c, ctx, c_ctx, w_mod, b_mod, norm1, norm2, ev_w_in, ev_a_conv, ev_qkv_conv, ev_a_log, ev_dt_bias, ev_o_norm, ev_w_out, od_w_in, od_q_norm, od_kv_norm, od_w_uq, od_w_ukv, od_w_out, p_w_q, p_keys, p_u, p_v, norm_f):
    b, s, d = x.shape
    assert ctx.shape[1] == TM and s % TM == 0
    c_rows = jnp.concatenate([c, c_ctx[None, :], jnp.zeros((16 - b - 1, d), F32)], axis=0)
    seq = jnp.concatenate([ctx, x], axis=1)
    joint = lambda bi, i: jnp.where(i == 0, b, bi)
    latent = lambda bi, i: bi

    mods = _modulation(c_rows, w_mod[0], b_mod[0])
    w_in = ev_w_in[0]
    proj, ab_t, edges = _norm_mod_proj(seq, norm1[0], mods, joint, 0, w_in[:, :3584].astype(BF16),
                                       w_t=w_in[:, 3584:].T, name="even_in")
    y_a, qkv = _even_prep(proj, edges, ev_a_conv[0], ev_qkv_conv[0])
    o_f, o_b = _gdn(qkv, ab_t, ev_a_log[0], ev_dt_bias[0])
    seq = _even_out(o_f, o_b, proj, y_a, seq, mods, joint, ev_o_norm[0], ev_w_out[0])
    seq = _peer_layer(seq, norm2[0], mods, joint, p_w_q[0], p_keys[0], p_u[0], p_v[0], norm_f, False)

    mods = _modulation(c_rows, w_mod[1], b_mod[1])
    w_in = od_w_in[0]
    k_rope = w_in[:, 640:704]
    w_ext = jnp.concatenate([w_in[:, :640], _pad_lanes(k_rope), _pad_lanes(_swap_cols(k_rope))], axis=1)
    (proj,) = _norm_mod_proj(seq, norm1[1], mods, joint, 0, w_ext.astype(BF16), name="mla_in")
    cos, sin = _rope_tables(s)
    q, k, v = _mla_expand(proj, od_q_norm[0], od_kv_norm[0], od_w_uq[0], od_w_ukv[0], cos, sin)
    o = _attention(q, k, v)
    xs = _attn_out(o, seq, mods, od_w_out[0])
    return _peer_layer(xs, norm2[1], mods, latent, p_w_q[1], p_keys[1], p_u[1], p_v[1], norm_f, True)
```

```python
import functools
import math

import jax
import jax.numpy as jnp
from jax import lax
from jax.experimental import pallas as pl
from jax.experimental.pallas import tpu as pltpu

F32 = jnp.float32
BF16 = jnp.bfloat16
HIGHEST = lax.Precision.HIGHEST

EPS = 1e-6
GRID_W = 64
ROPE_BASE = 10000.0
TM = 256
GDN_C = 128
GDN_HEADS = 4
C_HEADS = 8
C_NOPE = 128
C_ROPE = 64
P_HEADS = 8
P_NKEYS = 128
P_TOPK = 16
P_PAIRS = P_HEADS * P_TOPK
PEER_NT = 64
TOPK_TB = 256
ATTN_TQ = 512
VMEM_LIMIT = 48 * 1024 * 1024
PEER_VMEM = 50 * 1024 * 1024


def _nt_dot(a, b, precision=None):
    return lax.dot_general(a, b, (((1,), (1,)), ((), ())), precision=precision, preferred_element_type=F32)


def _tn_dot(a, b, precision=None):
    return lax.dot_general(a, b, (((0,), (0,)), ((), ())), precision=precision, preferred_element_type=F32)


def _dot(a, b, precision=None):
    return jnp.dot(a, b, precision=precision, preferred_element_type=F32)


_NN = (((1,), (0,)), ((), ()))
_NT = (((1,), (1,)), ((), ()))
_TN = (((0,), (0,)), ((), ()))


def _split2(x):
    hi = x.astype(BF16)
    return hi, (x - hi.astype(F32)).astype(BF16)


def _mm3(a, b, dims):
    f = lambda x, y: lax.dot_general(x, y, dims, preferred_element_type=F32)
    return f(a[0], b[0]) + (f(a[0], b[1]) + f(a[1], b[0]))


def _mm_exact(x, m, dims):
    hi = x.astype(BF16)
    r = x - hi.astype(F32)
    mid = r.astype(BF16)
    lo = (r - mid.astype(F32)).astype(BF16)
    f = lambda y: lax.dot_general(y, m, dims, preferred_element_type=F32)
    return f(hi) + (f(mid) + f(lo))


def _silu(x):
    return x * jax.nn.sigmoid(x)


def _params(*sem):
    return pltpu.CompilerParams(dimension_semantics=sem, vmem_limit_bytes=VMEM_LIMIT)


def _full(shape):
    return pl.BlockSpec(shape, lambda *_: (0,) * len(shape))


def _mod_kernel(c_ref, w_ref, b_ref, o_ref):
    o_ref[...] = _dot(_silu(c_ref[...]), w_ref[...], HIGHEST) + b_ref[...]


def _modulation(c_rows, w, b):
    r, d = c_rows.shape
    n = w.shape[1]
    tn = 512
    out = pl.pallas_call(
        _mod_kernel,
        grid=(n // tn,),
        in_specs=[_full((r, d)), pl.BlockSpec((d, tn), lambda j: (0, j)), pl.BlockSpec((1, tn), lambda j: (0, j))],
        out_specs=pl.BlockSpec((r, tn), lambda j: (0, j)),
        out_shape=jax.ShapeDtypeStruct((r, n), F32),
        compiler_params=_params("arbitrary"),
        name="modulation",
    )(c_rows, w, b.reshape(1, n))
    return out.reshape(r, 6, d)


def _proj_kernel(*refs, shift_row, highest, has_wt, emit_h):
    x_ref, g_ref, m_ref, w_ref = refs[:4]
    rest = list(refs[4:])
    wt_ref = rest.pop(0) if has_wt else None
    o_ref = rest.pop(0)
    x = x_ref[0]
    h = x * lax.rsqrt(jnp.mean(x * x, axis=-1, keepdims=True) + EPS) * g_ref[...]
    h = h * (1.0 + m_ref[0, shift_row + 1:shift_row + 2, :]) + m_ref[0, shift_row:shift_row + 1, :]
    if highest:
        o_ref[0] = _mm3(_split2(h), (w_ref[0], w_ref[1]), _NN).astype(o_ref.dtype)
    else:
        o_ref[0] = _dot(h.astype(BF16), w_ref[...]).astype(o_ref.dtype)
    if has_wt:
        rest.pop(0)[0] = _nt_dot(wt_ref[...], h, HIGHEST)
        edge_ref = rest.pop(0)
        edge_ref[0, 0, 0:1, :] = o_ref[0, 0:1, :]
        edge_ref[0, 0, 1:2, :] = o_ref[0, TM - 1:TM, :]
    if emit_h:
        rest.pop(0)[0] = h


def _norm_mod_proj(x3, gain, mods, mod_index, shift_row, w, *, highest=False, w_t=None, emit_h=False, name):
    b, l, d = x3.shape
    n = w.shape[1]
    if highest:
        w = jnp.stack(_split2(w))
    grid = (b, l // TM)
    in_specs = [pl.BlockSpec((1, TM, d), lambda bi, i: (bi, i, 0)),
                _full((1, d)),
                pl.BlockSpec((1, 6, d), lambda bi, i: (mod_index(bi, i), 0, 0)),
                _full(w.shape)]
    args = [x3, gain.reshape(1, d), mods, w]
    out_specs = [pl.BlockSpec((1, TM, n), lambda bi, i: (bi, i, 0))]
    out_shape = [jax.ShapeDtypeStruct((b, l, n), F32)]
    if w_t is not None:
        in_specs.append(_full(w_t.shape))
        args.append(w_t)
        out_specs.append(pl.BlockSpec((1, w_t.shape[0], TM), lambda bi, i: (bi, 0, i)))
        out_shape.append(jax.ShapeDtypeStruct((b, w_t.shape[0], l), F32))
        out_specs.append(pl.BlockSpec((1, 1, 2, n), lambda bi, i: (bi, i, 0, 0)))
        out_shape.append(jax.ShapeDtypeStruct((b, l // TM, 2, n), F32))
    if emit_h:
        out_specs.append(pl.BlockSpec((1, TM, d), lambda bi, i: (bi, i, 0)))
        out_shape.append(jax.ShapeDtypeStruct((b, l, d), F32))
    return pl.pallas_call(
        functools.partial(_proj_kernel, shift_row=shift_row, highest=highest, has_wt=w_t is not None, emit_h=emit_h),
        grid=grid, in_specs=in_specs, out_specs=out_specs, out_shape=out_shape,
        compiler_params=_params("arbitrary", "arbitrary"), name=name,
    )(*args)


def _even_prep_kernel(p_ref, halo_ref, aconv_ref, qconv_ref, ya_ref, qkv_ref):
    tm = p_ref.shape[1]
    rows = lax.broadcasted_iota(jnp.int32, (tm, 512), 0)
    first = rows == 0
    last = rows == tm - 1

    def conv(x, x_prev, x_next, w):
        xm = jnp.where(first, x_prev, pltpu.roll(x, 1, 0))
        xp = jnp.where(last, x_next, pltpu.roll(x, tm - 1, 0))
        return xm * w[0:1] + x * w[1:2] + xp * w[2:3]

    def cols(k):
        return p_ref[0, :, 512 * k:512 * (k + 1)]

    def halo(r, k):
        return halo_ref[0, 0, r:r + 1, 512 * k:512 * (k + 1)]

    ca = conv(cols(1) * cols(2), halo(0, 1) * halo(0, 2), halo(1, 1) * halo(1, 2), aconv_ref[...])
    ya_ref[0] = cols(0) * ca
    for s in range(3):
        y = _silu(conv(cols(3 + s), halo(0, 3 + s), halo(1, 3 + s), qconv_ref[:, 512 * s:512 * (s + 1)]))
        if s < 2:
            parts = []
            for h in range(GDN_HEADS):
                t = y[:, 128 * h:128 * (h + 1)]
                parts.append(t * lax.rsqrt(jnp.sum(t * t, axis=-1, keepdims=True) + EPS))
            y = jnp.concatenate(parts, axis=-1)
        qkv_ref[0, :, 512 * s:512 * (s + 1)] = y


def _even_prep(proj, edges, a_conv, qkv_conv):
    b, l, n = proj.shape
    nb = l // TM
    firstrows = edges[:, :, 0, :]
    lastrows = edges[:, :, 1, :]
    blk = jnp.arange(nb)
    prev = jnp.where((blk >= 2)[None, :, None], jnp.roll(lastrows, 1, axis=1), 0.0)
    nxt = jnp.where(((blk >= 1) & (blk < nb - 1))[None, :, None], jnp.roll(firstrows, -1, axis=1), 0.0)
    halo = jnp.stack([prev, nxt], axis=2)
    return pl.pallas_call(
        _even_prep_kernel,
        grid=(b, nb),
        in_specs=[pl.BlockSpec((1, TM, 3072), lambda bi, i: (bi, i, 0)),
                  pl.BlockSpec((1, 1, 2, n), lambda bi, i: (bi, i, 0, 0)),
                  _full(a_conv.shape), _full(qkv_conv.shape)],
        out_specs=[pl.BlockSpec((1, TM, 512), lambda bi, i: (bi, i, 0)),
                   pl.BlockSpec((1, TM, 1536), lambda bi, i: (bi, i, 0))],
        out_shape=[jax.ShapeDtypeStruct((b, l, 512), F32), jax.ShapeDtypeStruct((b, l, 1536), F32)],
        compiler_params=_params("arbitrary", "arbitrary"), name="even_prep",
    )(proj, halo, a_conv, qkv_conv)


def _gdn_kernel(qf_ref, qb_ref, abf_ref, abb_ref, alog_ref, dtb_ref, of_ref, ob_ref, s_ref):
    n = pl.program_id(1)
    c = GDN_C

    @pl.when(n == 0)
    def _():
        s_ref[...] = jnp.zeros_like(s_ref)

    row = lax.broadcasted_iota(jnp.int32, (c, c), 0)
    col = lax.broadcasted_iota(jnp.int32, (c, c), 1)
    eye = row == col
    eyef = jnp.where(eye, 1.0, 0.0)
    same = {b: (row // b) == (col // b) for b in (8, 16, 32, 64)}
    same[2 * 64] = row >= 0
    merge = {b: same[2 * b] & jnp.logical_not(same[b]) for b in (8, 16, 32, 64)}
    ones = jnp.ones((c, c), BF16)
    scale = 128 ** -0.5
    for d, (x_ref, ab_ref, o_ref) in enumerate(((qf_ref, abf_ref, of_ref), (qb_ref, abb_ref, ob_ref))):
        tri = (row >= col) if d == 0 else (row <= col)
        strict = (row > col) if d == 0 else (row < col)
        trif = jnp.where(tri, 1.0, 0.0).astype(BF16)
        ab = ab_ref[0]
        g4 = -jnp.exp(alog_ref[4 * d:4 * d + 4, :]) * jax.nn.softplus(ab[4 * d:4 * d + 4, :] + dtb_ref[4 * d:4 * d + 4, :])
        beta4 = jax.nn.sigmoid(ab[8 + 4 * d:12 + 4 * d, :])
        heads = range(GDN_HEADS)
        gt = [jnp.broadcast_to(g4[h:h + 1, :], (c, c)) for h in heads]
        bt = [jnp.broadcast_to(beta4[h:h + 1, :], (c, c)) for h in heads]
        dcol = [_mm_exact(jnp.where(tri, gt[h], 0.0), ones, _NN) for h in heads]
        drow = [_mm_exact(gt[h], trif, _NT) for h in heads]
        tot = [_mm_exact(gt[h], ones, _NN) for h in heads]
        bc = [_mm_exact(jnp.where(eye, bt[h], 0.0), ones, _NN) for h in heads]
        gamma = [jnp.exp(jnp.where(tri, dcol[h] - drow[h], -jnp.inf)) for h in heads]
        edec = [jnp.exp(dcol[h]) for h in heads]
        q = [x_ref[0, :, 128 * h:128 * (h + 1)] * scale for h in heads]
        k = [x_ref[0, :, 512 + 128 * h:512 + 128 * (h + 1)] for h in heads]
        v = [x_ref[0, :, 1024 + 128 * h:1024 + 128 * (h + 1)] for h in heads]
        kb = [k[h] * bc[h] for h in heads]
        k2 = [_split2(k[h]) for h in heads]
        a = [jnp.where(strict, _mm3(_split2(kb[h]), k2[h], _NT) * gamma[h], 0.0) for h in heads]
        d8 = [jnp.where(same[8], a[h], 0.0) for h in heads]
        t = [eyef - d8[h] for h in heads]
        p2 = [_split2(d8[h]) for h in heads]
        for _ in range(2):
            p2 = [_split2(_mm3(p2[h], p2[h], _NN)) for h in heads]
            t = [t[h] + _mm3(_split2(t[h]), p2[h], _NN) for h in heads]
        for blk in (8, 16, 32, 64):
            off = [jnp.where(merge[blk], a[h], 0.0) for h in heads]
            t2 = [_split2(t[h]) for h in heads]
            lt = [_mm3(_split2(off[h]), t2[h], _NN) for h in heads]
            t = [t[h] - _mm3(t2[h], _split2(lt[h]), _NN) for h in heads]
        t2 = [_split2(t[h]) for h in heads]
        u = [_mm3(t2[h], _split2(v[h] * bc[h]), _NN) for h in heads]
        w = [_mm3(t2[h], _split2(kb[h] * edec[h]), _NN) for h in heads]
        intra = [jnp.where(tri, _mm3(_split2(q[h]), k2[h], _NT) * gamma[h], 0.0) for h in heads]
        s = [s_ref[4 * d + h] for h in heads]
        s2 = [_split2(s[h]) for h in heads]
        v_new = [u[h] - _mm3(_split2(w[h]), s2[h], _NN) for h in heads]
        v2 = [_split2(v_new[h]) for h in heads]
        for h in heads:
            o_ref[0, :, 128 * h:128 * (h + 1)] = (_mm3(_split2(q[h] * edec[h]), s2[h], _NN)
                                                  + _mm3(_split2(intra[h]), v2[h], _NN))
            s_ref[4 * d + h] = s[h] * jnp.exp(tot[h]) + _mm3(_split2(k[h] * jnp.exp(tot[h] - dcol[h])), v2[h], _TN)


def _gdn(qkv, ab_t, a_log, dt_bias):
    b, l, _ = qkv.shape
    nb = l // GDN_C
    nctx = TM // GDN_C

    def bwd(n):
        return jnp.where(n < nctx, nctx - 1 - n, nb + nctx - 1 - n)

    alog = jnp.broadcast_to(a_log.reshape(8, 1), (8, 128))
    dtb = jnp.broadcast_to(dt_bias.reshape(8, 1), (8, 128))
    return pl.pallas_call(
        _gdn_kernel,
        grid=(b, nb),
        in_specs=[pl.BlockSpec((1, GDN_C, 1536), lambda bi, n: (bi, n, 0)),
                  pl.BlockSpec((1, GDN_C, 1536), lambda bi, n: (bi, bwd(n), 0)),
                  pl.BlockSpec((1, 16, GDN_C), lambda bi, n: (bi, 0, n)),
                  pl.BlockSpec((1, 16, GDN_C), lambda bi, n: (bi, 0, bwd(n))),
                  _full((8, 128)), _full((8, 128))],
        out_specs=[pl.BlockSpec((1, GDN_C, 512), lambda bi, n: (bi, n, 0)),
                   pl.BlockSpec((1, GDN_C, 512), lambda bi, n: (bi, bwd(n), 0))],
        out_shape=[jax.ShapeDtypeStruct((b, l, 512), F32), jax.ShapeDtypeStruct((b, l, 512), F32)],
        scratch_shapes=[pltpu.VMEM((2 * GDN_HEADS, 128, 128), F32)],
        compiler_params=_params("arbitrary", "arbitrary"), name="gdn",
    )(qkv, qkv, ab_t, ab_t, alog, dtb)


def _even_out_kernel(of_ref, ob_ref, z_ref, ya_ref, x_ref, m_ref, on_ref, w_ref, o_ref):
    o = of_ref[0] + ob_ref[0]
    z = z_ref[0]
    parts = []
    for h in range(GDN_HEADS):
        oh = o[:, 128 * h:128 * (h + 1)]
        y = oh * lax.rsqrt(jnp.mean(oh * oh, axis=-1, keepdims=True) + EPS) * on_ref[...]
        parts.append(y * _silu(z[:, 128 * h:128 * (h + 1)]))
    yb = jnp.concatenate(parts, axis=-1)
    y = _dot(ya_ref[0].astype(BF16), w_ref[0:512, :]) + _dot(yb.astype(BF16), w_ref[512:1536, :])
    o_ref[0] = x_ref[0] + m_ref[0, 2:3, :] * y


def _even_out(o_f, o_b, proj, y_a, x3, mods, mod_index, o_norm, w_out):
    b, l, d = x3.shape
    tok = lambda n: pl.BlockSpec((1, TM, n), lambda bi, i: (bi, i, 0))
    return pl.pallas_call(
        _even_out_kernel,
        grid=(b, l // TM),
        in_specs=[tok(512), tok(512),
                  pl.BlockSpec((1, TM, 512), lambda bi, i: (bi, i, 6)),
                  tok(512), tok(d),
                  pl.BlockSpec((1, 6, d), lambda bi, i: (mod_index(bi, i), 0, 0)),
                  _full((1, 128)), _full(w_out.shape)],
        out_specs=tok(d),
        out_shape=jax.ShapeDtypeStruct((b, l, d), F32),
        compiler_params=_params("arbitrary", "arbitrary"), name="even_out",
    )(o_f, o_b, proj, y_a, x3, mods, o_norm.reshape(1, 128), w_out.astype(BF16))


def _rope_tables(s):
    t = jnp.arange(s, dtype=jnp.int32)
    n = C_ROPE // 4
    freqs = ROPE_BASE ** (-jnp.arange(n, dtype=F32) / n)

    def seg(pos):
        ang = pos.astype(F32)[:, None] * freqs[None, :]
        return (jnp.concatenate([jnp.cos(ang), jnp.cos(ang)], -1), jnp.concatenate([-jnp.sin(ang), jnp.sin(ang)], -1))

    cr, sr = seg(t // GRID_W)
    cc, sc = seg(t % GRID_W)
    cos = jnp.concatenate([jnp.ones((TM, C_ROPE), F32), jnp.concatenate([cr, cc], -1)], 0)
    sin = jnp.concatenate([jnp.zeros((TM, C_ROPE), F32), jnp.concatenate([sr, sc], -1)], 0)
    return cos, sin


def _swap_cols(w):
    n = C_ROPE // 4
    perm = jnp.concatenate([jnp.arange(n) + n, jnp.arange(n), jnp.arange(n) + 3 * n, jnp.arange(n) + 2 * n])
    return w[..., perm]


def _pad_lanes(w, width=128):
    return jnp.concatenate([w, jnp.zeros(w.shape[:-1] + (width - w.shape[-1],), w.dtype)], axis=-1)


def _mla_expand_kernel(p_ref, qn_ref, kvn_ref, cos_ref, sin_ref, wq_ref, wqs_ref, wk_ref, wv_ref,
                       oq_ref, ok_ref, ov_ref):
    def norm(x, g):
        return (x * lax.rsqrt(jnp.mean(x * x, axis=-1, keepdims=True) + EPS) * g).astype(BF16)

    cq = norm(p_ref[0, :, 0:384], qn_ref[...])
    ckv = norm(p_ref[0, :, 384:640], kvn_ref[...])
    cos = cos_ref[...]
    sin = sin_ref[...]
    scale = (C_NOPE + C_ROPE) ** -0.5
    cos_h = jnp.concatenate([jnp.ones_like(cos), cos] * C_HEADS, axis=-1)
    sin_h = jnp.concatenate([jnp.zeros_like(sin), sin] * C_HEADS, axis=-1)
    oq_ref[0] = ((_dot(cq, wq_ref[...]) * cos_h + _dot(cq, wqs_ref[...]) * sin_h) * scale).astype(BF16)
    kn = _dot(ckv, wk_ref[...]).astype(BF16)
    kr = (p_ref[0, :, 640:768] * cos + p_ref[0, :, 768:896] * sin).astype(BF16)
    parts = []
    for h in range(C_HEADS):
        parts += [kn[:, 128 * h:128 * (h + 1)], kr]
    ok_ref[0] = jnp.concatenate(parts, axis=-1)
    ov_ref[0] = _dot(ckv, wv_ref[...]).astype(BF16)


def _mla_expand(proj, q_norm, kv_norm, w_uq, w_ukv, cos, sin):
    b, l, n = proj.shape
    hd = C_HEADS * 128
    wq = w_uq.reshape(-1, C_HEADS, C_NOPE + C_ROPE)
    rope = wq[:, :, C_NOPE:]
    w_q = jnp.concatenate([wq[:, :, :C_NOPE], _pad_lanes(rope)], axis=-1).reshape(-1, 2 * hd).astype(BF16)
    w_qs = jnp.concatenate([jnp.zeros_like(wq[:, :, :C_NOPE]), _pad_lanes(_swap_cols(rope))],
                           axis=-1).reshape(-1, 2 * hd).astype(BF16)
    wkv = w_ukv.reshape(-1, C_HEADS, 2 * 128)
    wk = wkv[:, :, :128].reshape(-1, hd).astype(BF16)
    wv = wkv[:, :, 128:].reshape(-1, hd).astype(BF16)
    tok = lambda m: pl.BlockSpec((1, TM, m), lambda bi, i: (bi, i, 0))
    tab = pl.BlockSpec((TM, 128), lambda bi, i: (i, 0))
    return pl.pallas_call(
        _mla_expand_kernel,
        grid=(b, l // TM),
        in_specs=[tok(n), _full((1, 384)), _full((1, 256)), tab, tab,
                  _full(w_q.shape), _full(w_qs.shape), _full(wk.shape), _full(wv.shape)],
        out_specs=[tok(2 * hd), tok(2 * hd), tok(hd)],
        out_shape=[jax.ShapeDtypeStruct((b, l, 2 * hd), BF16), jax.ShapeDtypeStruct((b, l, 2 * hd), BF16),
                   jax.ShapeDtypeStruct((b, l, hd), BF16)],
        compiler_params=_params("arbitrary", "arbitrary"), name="mla_expand",
    )(proj, q_norm.reshape(1, -1), kv_norm.reshape(1, -1), _pad_lanes(cos), _pad_lanes(sin), w_q, w_qs, wk, wv)


def _attn_kernel(qa_ref, qb_ref, k_ref, v_ref, o_ref):
    half = qa_ref.shape[1]
    rows = [slice(0, half), slice(half, 2 * half)]
    s = [_nt_dot(q[0], k_ref[0]) for q in (qa_ref, qb_ref)]
    m = [jnp.max(x, axis=-1, keepdims=True) for x in s]
    p = [jnp.exp(x - y) for x, y in zip(s, m)]
    den = [jnp.sum(x, axis=-1, keepdims=True) for x in p]
    o = [_dot(x.astype(BF16), v_ref[0]) for x in p]
    for r, x, y in zip(rows, o, den):
        o_ref[0, r, :] = (x / y).astype(o_ref.dtype)


def _attention(q, k, v):
    b, l, hd = v.shape
    s = l - TM
    half = ATTN_TQ // 2
    first = TM // half
    return pl.pallas_call(
        _attn_kernel,
        grid=(b, C_HEADS, s // ATTN_TQ),
        in_specs=[pl.BlockSpec((1, half, 256), lambda bi, h, i: (bi, 2 * i + first, h)),
                  pl.BlockSpec((1, half, 256), lambda bi, h, i: (bi, 2 * i + first + 1, h)),
                  pl.BlockSpec((1, l, 256), lambda bi, h, i: (bi, 0, h)),
                  pl.BlockSpec((1, l, 128), lambda bi, h, i: (bi, 0, h))],
        out_specs=pl.BlockSpec((1, ATTN_TQ, 128), lambda bi, h, i: (bi, i, h)),
        out_shape=jax.ShapeDtypeStruct((b, s, hd), BF16),
        compiler_params=_params("arbitrary", "arbitrary", "arbitrary"), name="attention",
    )(q, q, k, v)


def _attn_out_kernel(o_ref, x_ref, m_ref, w_ref, out_ref):
    out_ref[0] = x_ref[0] + m_ref[0, 2:3, :] * _dot(o_ref[0], w_ref[...])


def _attn_out(o, x3, mods, w_out):
    b, s, d = o.shape[0], o.shape[1], x3.shape[2]
    return pl.pallas_call(
        _attn_out_kernel,
        grid=(b, s // TM),
        in_specs=[pl.BlockSpec((1, TM, o.shape[2]), lambda bi, i: (bi, i, 0)),
                  pl.BlockSpec((1, TM, d), lambda bi, i: (bi, i + 1, 0)),
                  pl.BlockSpec((1, 6, d), lambda bi, i: (bi, 0, 0)),
                  _full(w_out.shape)],
        out_specs=pl.BlockSpec((1, TM, d), lambda bi, i: (bi, i, 0)),
        out_shape=jax.ShapeDtypeStruct((b, s, d), F32),
        compiler_params=_params("arbitrary", "arbitrary"), name="attn_out",
    )(o, x3, mods, w_out.astype(BF16))


def _topk_rows(s, k, ids=None):
    if ids is None:
        ids = lax.broadcasted_iota(jnp.int32, s.shape, 0)
    big = jnp.iinfo(jnp.int32).max
    vals, idxs = [], []
    for _ in range(k):
        m = jnp.max(s, axis=0, keepdims=True)
        am = jnp.min(jnp.where(s == m, ids, big), axis=0, keepdims=True)
        vals.append(m)
        idxs.append(am)
        s = jnp.where(ids == am, -jnp.inf, s)
    return vals, idxs


def _candidates(v1, v2):
    tl = v1[0].shape[1]
    v2s = jnp.concatenate(v2, axis=0)
    sub8 = lax.broadcasted_iota(jnp.int32, (8, tl), 0)
    sub16 = lax.broadcasted_iota(jnp.int32, (P_TOPK, tl), 0)
    vals = [v1[0] + v2s]
    ids = [sub16]
    for a in range(1, 5):
        nb = P_TOPK // (a + 1)
        vals.append(jnp.where(sub8 < nb, v1[a] + v2s[:8], -jnp.inf))
        ids.append(sub8 + P_TOPK * a)
    rest = [(a, b) for a in range(5, P_TOPK) for b in range(P_TOPK // (a + 1))]
    for group in (rest[:8], rest[8:]):
        rows = [v1[a] + v2[b] for a, b in group]
        rid = [jnp.full((1, tl), P_TOPK * a + b, jnp.int32) for a, b in group]
        pad = 8 - len(group)
        if pad:
            rows.append(jnp.full((pad, tl), -jnp.inf, F32))
            rid.append(jnp.full((pad, tl), P_TOPK * P_TOPK, jnp.int32))
        vals.append(jnp.concatenate(rows, axis=0))
        ids.append(jnp.concatenate(rid, axis=0))
    return jnp.concatenate(vals, axis=0), jnp.concatenate(ids, axis=0)


def _peer_topk_kernel(q_ref, key_ref, idx_ref, gate_ref):
    tl = 128
    for sub in range(q_ref.shape[0] // tl):
        for h in range(P_HEADS):
            q = q_ref[sub * tl:(sub + 1) * tl, 128 * h:128 * (h + 1)]
            st = _nt_dot(key_ref[h], q, HIGHEST)
            v1, i1 = _topk_rows(st[:P_NKEYS], P_TOPK)
            v2, i2 = _topk_rows(st[P_NKEYS:], P_TOPK)
            cand, cand_pos = _candidates(v1, v2)
            sc, pos = _topk_rows(cand, P_TOPK, cand_pos)
            sc = jnp.concatenate(sc, axis=0)
            pos = jnp.concatenate(pos, axis=0)
            pa = pos >> 4
            pb = pos & (P_TOPK - 1)
            e1 = jnp.zeros_like(pos)
            e2 = jnp.zeros_like(pos)
            for r in range(P_TOPK):
                e1 = jnp.where(pa == r, i1[r], e1)
                e2 = jnp.where(pb == r, i2[r], e2)
            ex = jnp.exp(sc - sc[0:1])
            idx_ref[P_TOPK * h:P_TOPK * (h + 1), sub * tl:(sub + 1) * tl] = (e1 * P_NKEYS + e2) * 4
            gate_ref[P_TOPK * h:P_TOPK * (h + 1), sub * tl:(sub + 1) * tl] = ex / jnp.sum(ex, axis=0, keepdims=True)


def _peer_topk(q, keys):
    t = q.shape[0]
    z = jnp.zeros((P_HEADS, P_NKEYS, 64), F32)
    kbd = jnp.concatenate([jnp.concatenate([keys[:, 0], z], -1), jnp.concatenate([z, keys[:, 1]], -1)], axis=1)
    return pl.pallas_call(
        _peer_topk_kernel,
        grid=(t // TOPK_TB,),
        in_specs=[pl.BlockSpec((TOPK_TB, 1024), lambda i: (i, 0)), _full(kbd.shape)],
        out_specs=[pl.BlockSpec((P_PAIRS, TOPK_TB), lambda i: (0, i)), pl.BlockSpec((P_PAIRS, TOPK_TB), lambda i: (0, i))],
        out_shape=[jax.ShapeDtypeStruct((P_PAIRS, t), jnp.int32), jax.ShapeDtypeStruct((P_PAIRS, t), F32)],
        compiler_params=_params("arbitrary"), name="peer_topk",
    )(q, kbd)


def _pack_kernel(x_ref, o_ref):
    eb = x_ref.shape[0]
    for r in range(4):
        halves = []
        for k in range(2):
            b = pltpu.bitcast(x_ref[:, 256 * r + 128 * k:256 * r + 128 * (k + 1)], jnp.uint32)
            halves.append((b + jnp.uint32(0x7FFF) + ((b >> 16) & jnp.uint32(1))) >> 16)
        o_ref[pl.ds(r, eb, stride=4), :] = pltpu.bitcast(halves[0] | (halves[1] << 16), jnp.int32)


def _pack_table(tab):
    e = tab.shape[0]
    eb = 256
    return pl.pallas_call(
        _pack_kernel,
        grid=(e // eb,),
        in_specs=[pl.BlockSpec((eb, 1024), lambda i: (i, 0))],
        out_specs=pl.BlockSpec((4 * eb, 128), lambda i: (i, 0)),
        out_shape=jax.ShapeDtypeStruct((4 * e, 128), jnp.int32),
        compiler_params=_params("arbitrary"), name="pack_table",
    )(tab)


def _gather_rows(idx_smem, tab_ref, tt, g_ref):
    for j in range(P_PAIRS):
        row = idx_smem[tt * P_PAIRS + j]
        g_ref[pl.ds(4 * j, 4), :] = tab_ref[pl.ds(pl.multiple_of(row, 4), 4), :]


def _index_stream(idx_hbm, bufs, sems, per_token):
    i = pl.program_id(0)
    n = pl.num_programs(0)

    def chunk(c, k):
        return pltpu.make_async_copy(idx_hbm.at[c], bufs[k], sems.at[k])

    @pl.when(i == 0)
    def _():
        chunk(0, 0).start()

    chunk(2 * i + 1, 1).start()
    chunk(2 * i, 0).wait()
    for tt in range(PEER_NT):
        per_token(tt, tt, bufs[0])

    @pl.when(i + 1 < n)
    def _():
        chunk(2 * i + 2, 0).start()

    chunk(2 * i + 1, 1).wait()
    for tt in range(PEER_NT):
        per_token(PEER_NT + tt, tt, bufs[1])


def _peer_act_kernel(idx_hbm, h_ref, gate_ref, tab_ref, w_ref, ia, ib, sems, g0, g1, rlo_ref, rhi_ref):
    half = 4 * P_PAIRS
    lane = lax.broadcasted_iota(jnp.int32, (16, half), 1)
    sub = lax.broadcasted_iota(jnp.int32, (16, half), 0)
    diag = (lane & 7) == (sub & 7)
    zero = jnp.zeros((8, 128), BF16)
    gbufs = (g0, g1)

    def per_token(t, tt, idx_smem):
        g_ref = gbufs[tt % 2]
        _gather_rows(idx_smem, tab_ref, tt, g_ref)
        gb = pltpu.bitcast(g_ref[...], BF16)
        g2 = jnp.concatenate([gb[:half], gb[half:]], axis=1)
        hb = jnp.concatenate([h_ref[t:t + 1, 128 * s:128 * (s + 1)] for s in range(8)], axis=0).astype(BF16)
        lhs = jnp.concatenate([jnp.concatenate([hb, zero], axis=1), jnp.concatenate([zero, hb], axis=1)], axis=0)
        r = jnp.where(diag, _nt_dot(lhs, g2), 0.0)
        rlo_ref[t:t + 1, :] = jnp.sum(r[:8], axis=0, keepdims=True)
        rhi_ref[t:t + 1, :] = jnp.sum(r[8:], axis=0, keepdims=True)

    _index_stream(idx_hbm, (ia, ib), sems, per_token)

    col = lax.broadcasted_iota(jnp.int32, (half, P_PAIRS), 0)
    pair = lax.broadcasted_iota(jnp.int32, (half, P_PAIRS), 1)

    def fold(r, first_pair):
        f = jnp.where((col >> 3) + first_pair == pair, 1.0, 0.0).astype(BF16)
        hi = r.astype(BF16)
        lo = (r - hi.astype(F32)).astype(BF16)
        return _dot(hi, f) + _dot(lo, f)

    act = fold(rlo_ref[...], 0) + fold(rhi_ref[...], P_PAIRS // 2)
    gelu = 0.5 * act * (1.0 + lax.erf(act * (1.0 / math.sqrt(2.0))))
    w_ref[...] = gate_ref[...] * gelu


def _peer_out_kernel(idx_hbm, w_ref, tab_ref, x_ref, m_ref, nf_ref, o_ref, ia, ib, sems, g0, g1,
                     whi_ref, wlo_ref, y_ref, *, final):
    lane = lax.broadcasted_iota(jnp.int32, (8, 8 * P_PAIRS), 1)
    sub = lax.broadcasted_iota(jnp.int32, (8, 8 * P_PAIRS), 0)
    diag = (lane & 7) == sub
    pair = lax.broadcasted_iota(jnp.int32, (P_PAIRS, 8 * P_PAIRS), 0)
    col = lax.broadcasted_iota(jnp.int32, (P_PAIRS, 8 * P_PAIRS), 1)
    spread = jnp.where((col >> 3) == pair, 1.0, 0.0).astype(BF16)
    w = w_ref[...]
    w_hi = w.astype(BF16)
    w_lo = (w - w_hi.astype(F32)).astype(BF16)
    whi_ref[...] = _dot(w_hi, spread)
    wlo_ref[...] = _dot(w_lo, spread)
    gbufs = (g0, g1)

    def per_token(t, tt, idx_smem):
        g_ref = gbufs[tt % 2]
        _gather_rows(idx_smem, tab_ref, tt, g_ref)
        gb = pltpu.bitcast(g_ref[...], BF16)
        rows = [jnp.where(diag, jnp.broadcast_to(r[t:t + 1, :], (8, 8 * P_PAIRS)), 0.0).astype(BF16)
                for r in (whi_ref, wlo_ref)]
        o = _dot(jnp.concatenate(rows, axis=0), gb)
        o = o[:8] + o[8:]
        for s in range(8):
            y_ref[t:t + 1, 128 * s:128 * (s + 1)] = o[s:s + 1, :]

    _index_stream(idx_hbm, (ia, ib), sems, per_token)

    y = x_ref[...] + m_ref[0, 5:6, :] * y_ref[...]
    if final:
        y = y * lax.rsqrt(jnp.mean(y * y, axis=-1, keepdims=True) + EPS) * nf_ref[...]
    o_ref[...] = y


def _peer_experts(x, h, idx4, gate, u_pack, v_pack, mods, mod_of_block, norm_f, final):
    t, d = h.shape
    tb = 2 * PEER_NT
    idx_chunks = idx4.reshape(t // PEER_NT, PEER_NT * P_PAIRS)
    grid = (t // tb,)
    idx_spec = pl.BlockSpec(memory_space=pl.ANY)
    tab_spec = pl.BlockSpec(memory_space=pltpu.VMEM)
    params = pltpu.CompilerParams(dimension_semantics=("arbitrary",), vmem_limit_bytes=PEER_VMEM)
    stream_scratch = [pltpu.SMEM((PEER_NT * P_PAIRS,), jnp.int32), pltpu.SMEM((PEER_NT * P_PAIRS,), jnp.int32),
                      pltpu.SemaphoreType.DMA((2,)),
                      pltpu.VMEM((4 * P_PAIRS, 128), jnp.int32), pltpu.VMEM((4 * P_PAIRS, 128), jnp.int32)]
    w = pl.pallas_call(
        _peer_act_kernel,
        grid=grid,
        in_specs=[idx_spec,
                  pl.BlockSpec((tb, d), lambda i: (i, 0)),
                  pl.BlockSpec((tb, P_PAIRS), lambda i: (i, 0)),
                  tab_spec],
        out_specs=pl.BlockSpec((tb, P_PAIRS), lambda i: (i, 0)),
        out_shape=jax.ShapeDtypeStruct((t, P_PAIRS), F32),
        scratch_shapes=stream_scratch + [pltpu.VMEM((tb, 4 * P_PAIRS), F32), pltpu.VMEM((tb, 4 * P_PAIRS), F32)],
        compiler_params=params,
        name="peer_act",
    )(idx_chunks, h, gate, u_pack)
    return pl.pallas_call(
        functools.partial(_peer_out_kernel, final=final),
        grid=grid,
        in_specs=[idx_spec,
                  pl.BlockSpec((tb, P_PAIRS), lambda i: (i, 0)),
                  tab_spec,
                  pl.BlockSpec((tb, d), lambda i: (i, 0)),
                  pl.BlockSpec((1, 6, d), lambda i: (mod_of_block(i), 0, 0)),
                  _full((1, d))],
        out_specs=pl.BlockSpec((tb, d), lambda i: (i, 0)),
        out_shape=jax.ShapeDtypeStruct((t, d), F32),
        scratch_shapes=stream_scratch + [pltpu.VMEM((tb, 8 * P_PAIRS), F32), pltpu.VMEM((tb, 8 * P_PAIRS), F32),
                                         pltpu.VMEM((tb, d), F32)],
        compiler_params=params,
        name="peer_out",
    )(idx_chunks, w, v_pack, x, mods, norm_f.reshape(1, d))


def _peer_layer(x3, norm2, mods, mod_index, w_q, keys, u_tab, v_tab, norm_f, final):
    b, l, d = x3.shape
    q, h = _norm_mod_proj(x3, norm2, mods, mod_index, 3, w_q, highest=True, emit_h=True, name="peer_query")
    idx_t, gate_t = _peer_topk(q.reshape(b * l, d), keys)
    per_seq = l // (2 * PEER_NT)
    mod_of_block = lambda i: mod_index(i // per_seq, (i % per_seq) // (TM // (2 * PEER_NT)))
    y = _peer_experts(x3.reshape(b * l, d), h.reshape(b * l, d), idx_t.T, gate_t.T,
                      _pack_table(u_tab), _pack_table(v_tab), mods, mod_of_block, norm_f, final)
    return y.reshape(b, l, d)


def kernel(x, c, ctx, c_ctx, w_mod, b_mod, norm1, norm2, ev_w_in, ev_a_conv, ev_qkv_conv, ev_a_log, ev_dt_bias, ev_o_norm, ev_w_out, od_w_in, od_q_norm, od_kv_norm, od_w_uq, od_w_ukv, od_w_out, p_w_q, p_keys, p_u, p_v, norm_f):
    b, s, d = x.shape
    assert ctx.shape[1] == TM and s % TM == 0
    c_rows = jnp.concatenate([c, c_ctx[None, :], jnp.zeros((16 - b - 1, d), F32)], axis=0)
    seq = jnp.concatenate([ctx, x], axis=1)
    joint = lambda bi, i: jnp.where(i == 0, b, bi)
    latent = lambda bi, i: bi

    mods = _modulation(c_rows, w_mod[0], b_mod[0])
    w_in = ev_w_in[0]
    proj, ab_t, edges = _norm_mod_proj(seq, norm1[0], mods, joint, 0, w_in[:, :3584].astype(BF16),
                                       w_t=w_in[:, 3584:].T, name="even_in")
    y_a, qkv = _even_prep(proj, edges, ev_a_conv[0], ev_qkv_conv[0])
    o_f, o_b = _gdn(qkv, ab_t, ev_a_log[0], ev_dt_bias[0])
    seq = _even_out(o_f, o_b, proj, y_a, seq, mods, joint, ev_o_norm[0], ev_w_out[0])
    seq = _peer_layer(seq, norm2[0], mods, joint, p_w_q[0], p_keys[0], p_u[0], p_v[0], norm_f, False)

    mods = _modulation(c_rows, w_mod[1], b_mod[1])
    w_in = od_w_in[0]
    k_rope = w_in[:, 640:704]
    w_ext = jnp.concatenate([w_in[:, :640], _pad_lanes(k_rope), _pad_lanes(_swap_cols(k_rope))], axis=1)
    (proj,) = _norm_mod_proj(seq, norm1[1], mods, joint, 0, w_ext.astype(BF16), name="mla_in")
    cos, sin = _rope_tables(s)
    q, k, v = _mla_expand(proj, od_q_norm[0], od_kv_norm[0], od_w_uq[0], od_w_ukv[0], cos, sin)
    o = _attention(q, k, v)
    xs = _attn_out(o, seq, mods, od_w_out[0])
    return _peer_layer(xs, norm2[1], mods, latent, p_w_q[1], p_keys[1], p_u[1], p_v[1], norm_f, True)
```

```python
import functools
import math

import jax
import jax.numpy as jnp
from jax import lax
from jax.experimental import pallas as pl
from jax.experimental.pallas import tpu as pltpu

F32 = jnp.float32
BF16 = jnp.bfloat16
HIGHEST = lax.Precision.HIGHEST

EPS = 1e-6
GRID_W = 64
ROPE_BASE = 10000.0
TM = 256
GDN_C = 128
GDN_HEADS = 4
C_HEADS = 8
C_NOPE = 128
C_ROPE = 64
P_HEADS = 8
P_NKEYS = 128
P_TOPK = 16
P_PAIRS = P_HEADS * P_TOPK
PEER_NT = 64
TOPK_TB = 256
ATTN_TQ = 512
ATTN_PARTS = 2
VMEM_LIMIT = 48 * 1024 * 1024
PEER_VMEM = 50 * 1024 * 1024


def _nt_dot(a, b, precision=None):
    return lax.dot_general(a, b, (((1,), (1,)), ((), ())), precision=precision, preferred_element_type=F32)


def _tn_dot(a, b, precision=None):
    return lax.dot_general(a, b, (((0,), (0,)), ((), ())), precision=precision, preferred_element_type=F32)


def _dot(a, b, precision=None):
    return jnp.dot(a, b, precision=precision, preferred_element_type=F32)


_NN = (((1,), (0,)), ((), ()))
_NT = (((1,), (1,)), ((), ()))
_TN = (((0,), (0,)), ((), ()))


def _split2(x):
    hi = x.astype(BF16)
    return hi, (x - hi.astype(F32)).astype(BF16)


def _mm3(a, b, dims):
    f = lambda x, y: lax.dot_general(x, y, dims, preferred_element_type=F32)
    return f(a[0], b[0]) + (f(a[0], b[1]) + f(a[1], b[0]))


def _mm_exact(x, m, dims):
    hi = x.astype(BF16)
    r = x - hi.astype(F32)
    mid = r.astype(BF16)
    lo = (r - mid.astype(F32)).astype(BF16)
    f = lambda y: lax.dot_general(y, m, dims, preferred_element_type=F32)
    return f(hi) + (f(mid) + f(lo))


def _silu(x):
    return x * jax.nn.sigmoid(x)


def _params(*sem):
    return pltpu.CompilerParams(dimension_semantics=sem, vmem_limit_bytes=VMEM_LIMIT)


def _full(shape):
    return pl.BlockSpec(shape, lambda *_: (0,) * len(shape))


def _mod_kernel(c_ref, w_ref, b_ref, o_ref):
    o_ref[...] = _dot(_silu(c_ref[...]), w_ref[...], HIGHEST) + b_ref[...]


def _modulation(c_rows, w, b):
    r, d = c_rows.shape
    n = w.shape[1]
    tn = 512
    out = pl.pallas_call(
        _mod_kernel,
        grid=(n // tn,),
        in_specs=[_full((r, d)), pl.BlockSpec((d, tn), lambda j: (0, j)), pl.BlockSpec((1, tn), lambda j: (0, j))],
        out_specs=pl.BlockSpec((r, tn), lambda j: (0, j)),
        out_shape=jax.ShapeDtypeStruct((r, n), F32),
        compiler_params=_params("arbitrary"),
        name="modulation",
    )(c_rows, w, b.reshape(1, n))
    return out.reshape(r, 6, d)


def _proj_kernel(*refs, shift_row, highest, has_wt, emit_h):
    x_ref, g_ref, m_ref, w_ref = refs[:4]
    rest = list(refs[4:])
    wt_ref = rest.pop(0) if has_wt else None
    o_ref = rest.pop(0)
    x = x_ref[0]
    h = x * lax.rsqrt(jnp.mean(x * x, axis=-1, keepdims=True) + EPS) * g_ref[...]
    h = h * (1.0 + m_ref[0, shift_row + 1:shift_row + 2, :]) + m_ref[0, shift_row:shift_row + 1, :]
    if highest:
        o_ref[0] = _mm3(_split2(h), (w_ref[0], w_ref[1]), _NN).astype(o_ref.dtype)
    else:
        o_ref[0] = _dot(h.astype(BF16), w_ref[...]).astype(o_ref.dtype)
    if has_wt:
        rest.pop(0)[0] = _nt_dot(wt_ref[...], h, HIGHEST)
        edge_ref = rest.pop(0)
        edge_ref[0, 0, 0:1, :] = o_ref[0, 0:1, :]
        edge_ref[0, 0, 1:2, :] = o_ref[0, TM - 1:TM, :]
    if emit_h:
        rest.pop(0)[0] = h


def _norm_mod_proj(x3, gain, mods, mod_index, shift_row, w, *, highest=False, w_t=None, emit_h=False, name):
    b, l, d = x3.shape
    n = w.shape[1]
    if highest:
        w = jnp.stack(_split2(w))
    grid = (b, l // TM)
    in_specs = [pl.BlockSpec((1, TM, d), lambda bi, i: (bi, i, 0)),
                _full((1, d)),
                pl.BlockSpec((1, 6, d), lambda bi, i: (mod_index(bi, i), 0, 0)),
                _full(w.shape)]
    args = [x3, gain.reshape(1, d), mods, w]
    out_specs = [pl.BlockSpec((1, TM, n), lambda bi, i: (bi, i, 0))]
    out_shape = [jax.ShapeDtypeStruct((b, l, n), F32)]
    if w_t is not None:
        in_specs.append(_full(w_t.shape))
        args.append(w_t)
        out_specs.append(pl.BlockSpec((1, w_t.shape[0], TM), lambda bi, i: (bi, 0, i)))
        out_shape.append(jax.ShapeDtypeStruct((b, w_t.shape[0], l), F32))
        out_specs.append(pl.BlockSpec((1, 1, 2, n), lambda bi, i: (bi, i, 0, 0)))
        out_shape.append(jax.ShapeDtypeStruct((b, l // TM, 2, n), F32))
    if emit_h:
        out_specs.append(pl.BlockSpec((1, TM, d), lambda bi, i: (bi, i, 0)))
        out_shape.append(jax.ShapeDtypeStruct((b, l, d), F32))
    return pl.pallas_call(
        functools.partial(_proj_kernel, shift_row=shift_row, highest=highest, has_wt=w_t is not None, emit_h=emit_h),
        grid=grid, in_specs=in_specs, out_specs=out_specs, out_shape=out_shape,
        compiler_params=_params("arbitrary", "arbitrary"), name=name,
    )(*args)


def _even_prep_kernel(p_ref, halo_ref, aconv_ref, qconv_ref, ya_ref, qkv_ref):
    tm = p_ref.shape[1]
    rows = lax.broadcasted_iota(jnp.int32, (tm, 512), 0)
    first = rows == 0
    last = rows == tm - 1

    def conv(x, x_prev, x_next, w):
        xm = jnp.where(first, x_prev, pltpu.roll(x, 1, 0))
        xp = jnp.where(last, x_next, pltpu.roll(x, tm - 1, 0))
        return xm * w[0:1] + x * w[1:2] + xp * w[2:3]

    def cols(k):
        return p_ref[0, :, 512 * k:512 * (k + 1)]

    def halo(r, k):
        return halo_ref[0, 0, r:r + 1, 512 * k:512 * (k + 1)]

    ca = conv(cols(1) * cols(2), halo(0, 1) * halo(0, 2), halo(1, 1) * halo(1, 2), aconv_ref[...])
    ya_ref[0] = cols(0) * ca
    for s in range(3):
        y = _silu(conv(cols(3 + s), halo(0, 3 + s), halo(1, 3 + s), qconv_ref[:, 512 * s:512 * (s + 1)]))
        if s < 2:
            parts = []
            for h in range(GDN_HEADS):
                t = y[:, 128 * h:128 * (h + 1)]
                parts.append(t * lax.rsqrt(jnp.sum(t * t, axis=-1, keepdims=True) + EPS))
            y = jnp.concatenate(parts, axis=-1)
        qkv_ref[0, :, 512 * s:512 * (s + 1)] = y


def _even_prep(proj, edges, a_conv, qkv_conv):
    b, l, n = proj.shape
    nb = l // TM
    firstrows = edges[:, :, 0, :]
    lastrows = edges[:, :, 1, :]
    blk = jnp.arange(nb)
    prev = jnp.where((blk >= 2)[None, :, None], jnp.roll(lastrows, 1, axis=1), 0.0)
    nxt = jnp.where(((blk >= 1) & (blk < nb - 1))[None, :, None], jnp.roll(firstrows, -1, axis=1), 0.0)
    halo = jnp.stack([prev, nxt], axis=2)
    return pl.pallas_call(
        _even_prep_kernel,
        grid=(b, nb),
        in_specs=[pl.BlockSpec((1, TM, 3072), lambda bi, i: (bi, i, 0)),
                  pl.BlockSpec((1, 1, 2, n), lambda bi, i: (bi, i, 0, 0)),
                  _full(a_conv.shape), _full(qkv_conv.shape)],
        out_specs=[pl.BlockSpec((1, TM, 512), lambda bi, i: (bi, i, 0)),
                   pl.BlockSpec((1, TM, 1536), lambda bi, i: (bi, i, 0))],
        out_shape=[jax.ShapeDtypeStruct((b, l, 512), F32), jax.ShapeDtypeStruct((b, l, 1536), F32)],
        compiler_params=_params("arbitrary", "arbitrary"), name="even_prep",
    )(proj, halo, a_conv, qkv_conv)


def _gdn_kernel(qf_ref, qb_ref, abf_ref, abb_ref, alog_ref, dtb_ref, of_ref, ob_ref, s_ref):
    n = pl.program_id(1)
    c = GDN_C

    @pl.when(n == 0)
    def _():
        s_ref[...] = jnp.zeros_like(s_ref)

    row = lax.broadcasted_iota(jnp.int32, (c, c), 0)
    col = lax.broadcasted_iota(jnp.int32, (c, c), 1)
    eye = row == col
    eyef = jnp.where(eye, 1.0, 0.0)
    same = {b: (row // b) == (col // b) for b in (8, 16, 32, 64)}
    same[2 * 64] = row >= 0
    merge = {b: same[2 * b] & jnp.logical_not(same[b]) for b in (8, 16, 32, 64)}
    ones = jnp.ones((c, c), BF16)
    scale = 128 ** -0.5
    x_refs = (qf_ref, qb_ref)
    o_refs = (of_ref, ob_ref)
    tri = ((row >= col), (row <= col))
    strict = ((row > col), (row < col))
    trif = [jnp.where(m, 1.0, 0.0).astype(BF16) for m in tri]
    g4, beta4 = [], []
    for d, ab_ref in enumerate((abf_ref, abb_ref)):
        ab = ab_ref[0]
        g4.append(-jnp.exp(alog_ref[4 * d:4 * d + 4, :])
                  * jax.nn.softplus(ab[4 * d:4 * d + 4, :] + dtb_ref[4 * d:4 * d + 4, :]))
        beta4.append(jax.nn.sigmoid(ab[8 + 4 * d:12 + 4 * d, :]))
    chain = [(d, h) for d in range(2) for h in range(GDN_HEADS)]
    ids = range(len(chain))
    gt = [jnp.broadcast_to(g4[d][h:h + 1, :], (c, c)) for d, h in chain]
    bt = [jnp.broadcast_to(beta4[d][h:h + 1, :], (c, c)) for d, h in chain]
    dcol = [_mm_exact(jnp.where(tri[chain[i][0]], gt[i], 0.0), ones, _NN) for i in ids]
    drow = [_mm_exact(gt[i], trif[chain[i][0]], _NT) for i in ids]
    tot = [_mm_exact(gt[i], ones, _NN) for i in ids]
    bc = [_mm_exact(jnp.where(eye, bt[i], 0.0), ones, _NN) for i in ids]
    gamma = [jnp.exp(jnp.where(tri[chain[i][0]], dcol[i] - drow[i], -jnp.inf)) for i in ids]
    edec = [jnp.exp(dcol[i]) for i in ids]
    q = [x_refs[d][0, :, 128 * h:128 * (h + 1)] * scale for d, h in chain]
    k = [x_refs[d][0, :, 512 + 128 * h:512 + 128 * (h + 1)] for d, h in chain]
    v = [x_refs[d][0, :, 1024 + 128 * h:1024 + 128 * (h + 1)] for d, h in chain]
    kb = [k[i] * bc[i] for i in ids]
    k2 = [_split2(k[i]) for i in ids]
    a = [jnp.where(strict[chain[i][0]], _mm3(_split2(kb[i]), k2[i], _NT) * gamma[i], 0.0) for i in ids]
    d8 = [jnp.where(same[8], a[i], 0.0) for i in ids]
    t = [eyef - d8[i] for i in ids]
    p2 = [_split2(d8[i]) for i in ids]
    for _ in range(2):
        p2 = [_split2(_mm3(p2[i], p2[i], _NN)) for i in ids]
        t = [t[i] + _mm3(_split2(t[i]), p2[i], _NN) for i in ids]
    for blk in (8, 16, 32, 64):
        off = [jnp.where(merge[blk], a[i], 0.0) for i in ids]
        t2 = [_split2(t[i]) for i in ids]
        lt = [_mm3(_split2(off[i]), t2[i], _NN) for i in ids]
        t = [t[i] - _mm3(t2[i], _split2(lt[i]), _NN) for i in ids]
    t2 = [_split2(t[i]) for i in ids]
    u = [_mm3(t2[i], _split2(v[i] * bc[i]), _NN) for i in ids]
    w = [_mm3(t2[i], _split2(kb[i] * edec[i]), _NN) for i in ids]
    intra = [jnp.where(tri[chain[i][0]], _mm3(_split2(q[i]), k2[i], _NT) * gamma[i], 0.0) for i in ids]
    s = [s_ref[i] for i in ids]
    s2 = [_split2(s[i]) for i in ids]
    v_new = [u[i] - _mm3(_split2(w[i]), s2[i], _NN) for i in ids]
    v2 = [_split2(v_new[i]) for i in ids]
    for i, (d, h) in enumerate(chain):
        o_refs[d][0, :, 128 * h:128 * (h + 1)] = (_mm3(_split2(q[i] * edec[i]), s2[i], _NN)
                                                  + _mm3(_split2(intra[i]), v2[i], _NN))
        s_ref[i] = s[i] * jnp.exp(tot[i]) + _mm3(_split2(k[i] * jnp.exp(tot[i] - dcol[i])), v2[i], _TN)


def _gdn(qkv, ab_t, a_log, dt_bias):
    b, l, _ = qkv.shape
    nb = l // GDN_C
    nctx = TM // GDN_C

    def bwd(n):
        return jnp.where(n < nctx, nctx - 1 - n, nb + nctx - 1 - n)

    alog = jnp.broadcast_to(a_log.reshape(8, 1), (8, 128))
    dtb = jnp.broadcast_to(dt_bias.reshape(8, 1), (8, 128))
    return pl.pallas_call(
        _gdn_kernel,
        grid=(b, nb),
        in_specs=[pl.BlockSpec((1, GDN_C, 1536), lambda bi, n: (bi, n, 0)),
                  pl.BlockSpec((1, GDN_C, 1536), lambda bi, n: (bi, bwd(n), 0)),
                  pl.BlockSpec((1, 16, GDN_C), lambda bi, n: (bi, 0, n)),
                  pl.BlockSpec((1, 16, GDN_C), lambda bi, n: (bi, 0, bwd(n))),
                  _full((8, 128)), _full((8, 128))],
        out_specs=[pl.BlockSpec((1, GDN_C, 512), lambda bi, n: (bi, n, 0)),
                   pl.BlockSpec((1, GDN_C, 512), lambda bi, n: (bi, bwd(n), 0))],
        out_shape=[jax.ShapeDtypeStruct((b, l, 512), F32), jax.ShapeDtypeStruct((b, l, 512), F32)],
        scratch_shapes=[pltpu.VMEM((2 * GDN_HEADS, 128, 128), F32)],
        compiler_params=_params("arbitrary", "arbitrary"), name="gdn",
    )(qkv, qkv, ab_t, ab_t, alog, dtb)


def _even_out_kernel(of_ref, ob_ref, z_ref, ya_ref, x_ref, m_ref, on_ref, w_ref, o_ref):
    o = of_ref[0] + ob_ref[0]
    z = z_ref[0]
    parts = []
    for h in range(GDN_HEADS):
        oh = o[:, 128 * h:128 * (h + 1)]
        y = oh * lax.rsqrt(jnp.mean(oh * oh, axis=-1, keepdims=True) + EPS) * on_ref[...]
        parts.append(y * _silu(z[:, 128 * h:128 * (h + 1)]))
    yb = jnp.concatenate(parts, axis=-1)
    y = _dot(ya_ref[0].astype(BF16), w_ref[0:512, :]) + _dot(yb.astype(BF16), w_ref[512:1536, :])
    o_ref[0] = x_ref[0] + m_ref[0, 2:3, :] * y


def _even_out(o_f, o_b, proj, y_a, x3, mods, mod_index, o_norm, w_out):
    b, l, d = x3.shape
    tok = lambda n: pl.BlockSpec((1, TM, n), lambda bi, i: (bi, i, 0))
    return pl.pallas_call(
        _even_out_kernel,
        grid=(b, l // TM),
        in_specs=[tok(512), tok(512),
                  pl.BlockSpec((1, TM, 512), lambda bi, i: (bi, i, 6)),
                  tok(512), tok(d),
                  pl.BlockSpec((1, 6, d), lambda bi, i: (mod_index(bi, i), 0, 0)),
                  _full((1, 128)), _full(w_out.shape)],
        out_specs=tok(d),
        out_shape=jax.ShapeDtypeStruct((b, l, d), F32),
        compiler_params=_params("arbitrary", "arbitrary"), name="even_out",
    )(o_f, o_b, proj, y_a, x3, mods, o_norm.reshape(1, 128), w_out.astype(BF16))


def _rope_tables(s):
    t = jnp.arange(s, dtype=jnp.int32)
    n = C_ROPE // 4
    freqs = ROPE_BASE ** (-jnp.arange(n, dtype=F32) / n)

    def seg(pos):
        ang = pos.astype(F32)[:, None] * freqs[None, :]
        return (jnp.concatenate([jnp.cos(ang), jnp.cos(ang)], -1), jnp.concatenate([-jnp.sin(ang), jnp.sin(ang)], -1))

    cr, sr = seg(t // GRID_W)
    cc, sc = seg(t % GRID_W)
    cos = jnp.concatenate([jnp.ones((TM, C_ROPE), F32), jnp.concatenate([cr, cc], -1)], 0)
    sin = jnp.concatenate([jnp.zeros((TM, C_ROPE), F32), jnp.concatenate([sr, sc], -1)], 0)
    return cos, sin


def _swap_cols(w):
    n = C_ROPE // 4
    perm = jnp.concatenate([jnp.arange(n) + n, jnp.arange(n), jnp.arange(n) + 3 * n, jnp.arange(n) + 2 * n])
    return w[..., perm]


def _pad_lanes(w, width=128):
    return jnp.concatenate([w, jnp.zeros(w.shape[:-1] + (width - w.shape[-1],), w.dtype)], axis=-1)


def _mla_expand_kernel(p_ref, qn_ref, kvn_ref, cos_ref, sin_ref, wq_ref, wqs_ref, wk_ref, wv_ref,
                       oq_ref, ok_ref, ov_ref):
    def norm(x, g):
        return (x * lax.rsqrt(jnp.mean(x * x, axis=-1, keepdims=True) + EPS) * g).astype(BF16)

    cq = norm(p_ref[0, :, 0:384], qn_ref[...])
    ckv = norm(p_ref[0, :, 384:640], kvn_ref[...])
    cos = cos_ref[...]
    sin = sin_ref[...]
    scale = (C_NOPE + C_ROPE) ** -0.5
    cos_h = jnp.concatenate([jnp.ones_like(cos), cos] * C_HEADS, axis=-1)
    sin_h = jnp.concatenate([jnp.zeros_like(sin), sin] * C_HEADS, axis=-1)
    oq_ref[0] = ((_dot(cq, wq_ref[...]) * cos_h + _dot(cq, wqs_ref[...]) * sin_h) * scale).astype(BF16)
    kn = _dot(ckv, wk_ref[...]).astype(BF16)
    kr = (p_ref[0, :, 640:768] * cos + p_ref[0, :, 768:896] * sin).astype(BF16)
    parts = []
    for h in range(C_HEADS):
        parts += [kn[:, 128 * h:128 * (h + 1)], kr]
    ok_ref[0] = jnp.concatenate(parts, axis=-1)
    ov_ref[0] = _dot(ckv, wv_ref[...]).astype(BF16)


def _mla_expand(proj, q_norm, kv_norm, w_uq, w_ukv, cos, sin):
    b, l, n = proj.shape
    hd = C_HEADS * 128
    wq = w_uq.reshape(-1, C_HEADS, C_NOPE + C_ROPE)
    rope = wq[:, :, C_NOPE:]
    w_q = jnp.concatenate([wq[:, :, :C_NOPE], _pad_lanes(rope)], axis=-1).reshape(-1, 2 * hd).astype(BF16)
    w_qs = jnp.concatenate([jnp.zeros_like(wq[:, :, :C_NOPE]), _pad_lanes(_swap_cols(rope))],
                           axis=-1).reshape(-1, 2 * hd).astype(BF16)
    wkv = w_ukv.reshape(-1, C_HEADS, 2 * 128)
    wk = wkv[:, :, :128].reshape(-1, hd).astype(BF16)
    wv = wkv[:, :, 128:].reshape(-1, hd).astype(BF16)
    tok = lambda m: pl.BlockSpec((1, TM, m), lambda bi, i: (bi, i, 0))
    tab = pl.BlockSpec((TM, 128), lambda bi, i: (i, 0))
    return pl.pallas_call(
        _mla_expand_kernel,
        grid=(b, l // TM),
        in_specs=[tok(n), _full((1, 384)), _full((1, 256)), tab, tab,
                  _full(w_q.shape), _full(w_qs.shape), _full(wk.shape), _full(wv.shape)],
        out_specs=[tok(2 * hd), tok(2 * hd), tok(hd)],
        out_shape=[jax.ShapeDtypeStruct((b, l, 2 * hd), BF16), jax.ShapeDtypeStruct((b, l, 2 * hd), BF16),
                   jax.ShapeDtypeStruct((b, l, hd), BF16)],
        compiler_params=_params("arbitrary", "arbitrary"), name="mla_expand",
    )(proj, q_norm.reshape(1, -1), kv_norm.reshape(1, -1), _pad_lanes(cos), _pad_lanes(sin), w_q, w_qs, wk, wv)


def _attn_kernel(*refs):
    q_refs, (k_ref, v_ref, o_ref) = refs[:ATTN_PARTS], refs[ATTN_PARTS:]
    rows = q_refs[0].shape[1]
    s = [_nt_dot(q[0], k_ref[0]) for q in q_refs]
    m = [jnp.max(x, axis=-1, keepdims=True) for x in s]
    p = [jnp.exp(x - y) for x, y in zip(s, m)]
    den = [jnp.sum(x, axis=-1, keepdims=True) for x in p]
    o = [_dot(x.astype(BF16), v_ref[0]) for x in p]
    for j, (x, y) in enumerate(zip(o, den)):
        o_ref[0, j * rows:(j + 1) * rows, :] = (x / y).astype(o_ref.dtype)


def _attention(q, k, v):
    b, l, hd = v.shape
    s = l - TM
    rows = ATTN_TQ // ATTN_PARTS
    first = TM // rows

    def part(j):
        return pl.BlockSpec((1, rows, 256), lambda bi, h, i: (bi, ATTN_PARTS * i + first + j, h))

    return pl.pallas_call(
        _attn_kernel,
        grid=(b, C_HEADS, s // ATTN_TQ),
        in_specs=[part(j) for j in range(ATTN_PARTS)]
        + [pl.BlockSpec((1, l, 256), lambda bi, h, i: (bi, 0, h)),
           pl.BlockSpec((1, l, 128), lambda bi, h, i: (bi, 0, h))],
        out_specs=pl.BlockSpec((1, ATTN_TQ, 128), lambda bi, h, i: (bi, i, h)),
        out_shape=jax.ShapeDtypeStruct((b, s, hd), BF16),
        compiler_params=_params("arbitrary", "arbitrary", "arbitrary"), name="attention",
    )(*([q] * ATTN_PARTS), k, v)


def _attn_out_kernel(o_ref, x_ref, m_ref, w_ref, out_ref):
    out_ref[0] = x_ref[0] + m_ref[0, 2:3, :] * _dot(o_ref[0], w_ref[...])


def _attn_out(o, x3, mods, w_out):
    b, s, d = o.shape[0], o.shape[1], x3.shape[2]
    return pl.pallas_call(
        _attn_out_kernel,
        grid=(b, s // TM),
        in_specs=[pl.BlockSpec((1, TM, o.shape[2]), lambda bi, i: (bi, i, 0)),
                  pl.BlockSpec((1, TM, d), lambda bi, i: (bi, i + 1, 0)),
                  pl.BlockSpec((1, 6, d), lambda bi, i: (bi, 0, 0)),
                  _full(w_out.shape)],
        out_specs=pl.BlockSpec((1, TM, d), lambda bi, i: (bi, i, 0)),
        out_shape=jax.ShapeDtypeStruct((b, s, d), F32),
        compiler_params=_params("arbitrary", "arbitrary"), name="attn_out",
    )(o, x3, mods, w_out.astype(BF16))


def _topk_rows(s, k, ids=None):
    if ids is None:
        ids = lax.broadcasted_iota(jnp.int32, s.shape, 0)
    big = jnp.iinfo(jnp.int32).max
    vals, idxs = [], []
    for _ in range(k):
        m = jnp.max(s, axis=0, keepdims=True)
        am = jnp.min(jnp.where(s == m, ids, big), axis=0, keepdims=True)
        vals.append(m)
        idxs.append(am)
        s = jnp.where(ids == am, -jnp.inf, s)
    return vals, idxs


def _candidates(v1, v2):
    tl = v1[0].shape[1]
    v2s = jnp.concatenate(v2, axis=0)
    sub8 = lax.broadcasted_iota(jnp.int32, (8, tl), 0)
    sub16 = lax.broadcasted_iota(jnp.int32, (P_TOPK, tl), 0)
    vals = [v1[0] + v2s]
    ids = [sub16]
    for a in range(1, 5):
        nb = P_TOPK // (a + 1)
        vals.append(jnp.where(sub8 < nb, v1[a] + v2s[:8], -jnp.inf))
        ids.append(sub8 + P_TOPK * a)
    rest = [(a, b) for a in range(5, P_TOPK) for b in range(P_TOPK // (a + 1))]
    for group in (rest[:8], rest[8:]):
        rows = [v1[a] + v2[b] for a, b in group]
        rid = [jnp.full((1, tl), P_TOPK * a + b, jnp.int32) for a, b in group]
        pad = 8 - len(group)
        if pad:
            rows.append(jnp.full((pad, tl), -jnp.inf, F32))
            rid.append(jnp.full((pad, tl), P_TOPK * P_TOPK, jnp.int32))
        vals.append(jnp.concatenate(rows, axis=0))
        ids.append(jnp.concatenate(rid, axis=0))
    return jnp.concatenate(vals, axis=0), jnp.concatenate(ids, axis=0)


def _peer_topk_kernel(q_ref, key_ref, idx_ref, gate_ref):
    tl = 128
    for sub in range(q_ref.shape[0] // tl):
        for h in range(P_HEADS):
            q = q_ref[sub * tl:(sub + 1) * tl, 128 * h:128 * (h + 1)]
            st = _nt_dot(key_ref[h], q, HIGHEST)
            v1, i1 = _topk_rows(st[:P_NKEYS], P_TOPK)
            v2, i2 = _topk_rows(st[P_NKEYS:], P_TOPK)
            cand, cand_pos = _candidates(v1, v2)
            sc, pos = _topk_rows(cand, P_TOPK, cand_pos)
            sc = jnp.concatenate(sc, axis=0)
            pos = jnp.concatenate(pos, axis=0)
            pa = pos >> 4
            pb = pos & (P_TOPK - 1)
            e1 = jnp.zeros_like(pos)
            e2 = jnp.zeros_like(pos)
            for r in range(P_TOPK):
                e1 = jnp.where(pa == r, i1[r], e1)
                e2 = jnp.where(pb == r, i2[r], e2)
            ex = jnp.exp(sc - sc[0:1])
            idx_ref[P_TOPK * h:P_TOPK * (h + 1), sub * tl:(sub + 1) * tl] = (e1 * P_NKEYS + e2) * 4
            gate_ref[P_TOPK * h:P_TOPK * (h + 1), sub * tl:(sub + 1) * tl] = ex / jnp.sum(ex, axis=0, keepdims=True)


def _peer_topk(q, keys):
    t = q.shape[0]
    z = jnp.zeros((P_HEADS, P_NKEYS, 64), F32)
    kbd = jnp.concatenate([jnp.concatenate([keys[:, 0], z], -1), jnp.concatenate([z, keys[:, 1]], -1)], axis=1)
    return pl.pallas_call(
        _peer_topk_kernel,
        grid=(t // TOPK_TB,),
        in_specs=[pl.BlockSpec((TOPK_TB, 1024), lambda i: (i, 0)), _full(kbd.shape)],
        out_specs=[pl.BlockSpec((P_PAIRS, TOPK_TB), lambda i: (0, i)), pl.BlockSpec((P_PAIRS, TOPK_TB), lambda i: (0, i))],
        out_shape=[jax.ShapeDtypeStruct((P_PAIRS, t), jnp.int32), jax.ShapeDtypeStruct((P_PAIRS, t), F32)],
        compiler_params=_params("arbitrary"), name="peer_topk",
    )(q, kbd)


def _pack_kernel(x_ref, o_ref):
    eb = x_ref.shape[0]
    for r in range(4):
        halves = []
        for k in range(2):
            b = pltpu.bitcast(x_ref[:, 256 * r + 128 * k:256 * r + 128 * (k + 1)], jnp.uint32)
            halves.append((b + jnp.uint32(0x7FFF) + ((b >> 16) & jnp.uint32(1))) >> 16)
        o_ref[pl.ds(r, eb, stride=4), :] = pltpu.bitcast(halves[0] | (halves[1] << 16), jnp.int32)


def _pack_table(tab):
    e = tab.shape[0]
    eb = 256
    return pl.pallas_call(
        _pack_kernel,
        grid=(e // eb,),
        in_specs=[pl.BlockSpec((eb, 1024), lambda i: (i, 0))],
        out_specs=pl.BlockSpec((4 * eb, 128), lambda i: (i, 0)),
        out_shape=jax.ShapeDtypeStruct((4 * e, 128), jnp.int32),
        compiler_params=_params("arbitrary"), name="pack_table",
    )(tab)


def _gather_rows(idx_smem, tab_ref, tt, g_ref):
    for j in range(P_PAIRS):
        row = idx_smem[tt * P_PAIRS + j]
        g_ref[pl.ds(4 * j, 4), :] = tab_ref[pl.ds(pl.multiple_of(row, 4), 4), :]


def _index_stream(idx_hbm, bufs, sems, per_token):
    i = pl.program_id(0)
    n = pl.num_programs(0)

    def chunk(c, k):
        return pltpu.make_async_copy(idx_hbm.at[c], bufs[k], sems.at[k])

    @pl.when(i == 0)
    def _():
        chunk(0, 0).start()

    chunk(2 * i + 1, 1).start()
    chunk(2 * i, 0).wait()
    for tt in range(PEER_NT):
        per_token(tt, tt, bufs[0])

    @pl.when(i + 1 < n)
    def _():
        chunk(2 * i + 2, 0).start()

    chunk(2 * i + 1, 1).wait()
    for tt in range(PEER_NT):
        per_token(PEER_NT + tt, tt, bufs[1])


def _peer_act_kernel(idx_hbm, h_ref, gate_ref, tab_ref, w_ref, ia, ib, sems, g0, g1, rlo_ref, rhi_ref):
    half = 4 * P_PAIRS
    lane = lax.broadcasted_iota(jnp.int32, (16, half), 1)
    sub = lax.broadcasted_iota(jnp.int32, (16, half), 0)
    diag = (lane & 7) == (sub & 7)
    zero = jnp.zeros((8, 128), BF16)
    gbufs = (g0, g1)

    def per_token(t, tt, idx_smem):
        g_ref = gbufs[tt % 2]
        _gather_rows(idx_smem, tab_ref, tt, g_ref)
        gb = pltpu.bitcast(g_ref[...], BF16)
        g2 = jnp.concatenate([gb[:half], gb[half:]], axis=1)
        hb = jnp.concatenate([h_ref[t:t + 1, 128 * s:128 * (s + 1)] for s in range(8)], axis=0).astype(BF16)
        lhs = jnp.concatenate([jnp.concatenate([hb, zero], axis=1), jnp.concatenate([zero, hb], axis=1)], axis=0)
        r = jnp.where(diag, _nt_dot(lhs, g2), 0.0)
        rlo_ref[t:t + 1, :] = jnp.sum(r[:8], axis=0, keepdims=True)
        rhi_ref[t:t + 1, :] = jnp.sum(r[8:], axis=0, keepdims=True)

    _index_stream(idx_hbm, (ia, ib), sems, per_token)

    col = lax.broadcasted_iota(jnp.int32, (half, P_PAIRS), 0)
    pair = lax.broadcasted_iota(jnp.int32, (half, P_PAIRS), 1)

    def fold(r, first_pair):
        f = jnp.where((col >> 3) + first_pair == pair, 1.0, 0.0).astype(BF16)
        hi = r.astype(BF16)
        lo = (r - hi.astype(F32)).astype(BF16)
        return _dot(hi, f) + _dot(lo, f)

    act = fold(rlo_ref[...], 0) + fold(rhi_ref[...], P_PAIRS // 2)
    gelu = 0.5 * act * (1.0 + lax.erf(act * (1.0 / math.sqrt(2.0))))
    w_ref[...] = gate_ref[...] * gelu


def _peer_out_kernel(idx_hbm, w_ref, tab_ref, x_ref, m_ref, nf_ref, o_ref, ia, ib, sems, g0, g1,
                     whi_ref, wlo_ref, y_ref, *, final):
    lane = lax.broadcasted_iota(jnp.int32, (8, 8 * P_PAIRS), 1)
    sub = lax.broadcasted_iota(jnp.int32, (8, 8 * P_PAIRS), 0)
    diag = (lane & 7) == sub
    pair = lax.broadcasted_iota(jnp.int32, (P_PAIRS, 8 * P_PAIRS), 0)
    col = lax.broadcasted_iota(jnp.int32, (P_PAIRS, 8 * P_PAIRS), 1)
    spread = jnp.where((col >> 3) == pair, 1.0, 0.0).astype(BF16)
    w = w_ref[...]
    w_hi = w.astype(BF16)
    w_lo = (w - w_hi.astype(F32)).astype(BF16)
    whi_ref[...] = _dot(w_hi, spread)
    wlo_ref[...] = _dot(w_lo, spread)
    gbufs = (g0, g1)

    def per_token(t, tt, idx_smem):
        g_ref = gbufs[tt % 2]
        _gather_rows(idx_smem, tab_ref, tt, g_ref)
        gb = pltpu.bitcast(g_ref[...], BF16)
        rows = [jnp.where(diag, jnp.broadcast_to(r[t:t + 1, :], (8, 8 * P_PAIRS)), 0.0).astype(BF16)
                for r in (whi_ref, wlo_ref)]
        o = _dot(jnp.concatenate(rows, axis=0), gb)
        o = o[:8] + o[8:]
        for s in range(8):
            y_ref[t:t + 1, 128 * s:128 * (s + 1)] = o[s:s + 1, :]

    _index_stream(idx_hbm, (ia, ib), sems, per_token)

    y = x_ref[...] + m_ref[0, 5:6, :] * y_ref[...]
    if final:
        y = y * lax.rsqrt(jnp.mean(y * y, axis=-1, keepdims=True) + EPS) * nf_ref[...]
    o_ref[...] = y


def _peer_experts(x, h, idx4, gate, u_pack, v_pack, mods, mod_of_block, norm_f, final):
    t, d = h.shape
    tb = 2 * PEER_NT
    idx_chunks = idx4.reshape(t // PEER_NT, PEER_NT * P_PAIRS)
    grid = (t // tb,)
    idx_spec = pl.BlockSpec(memory_space=pl.ANY)
    tab_spec = pl.BlockSpec(memory_space=pltpu.VMEM)
    params = pltpu.CompilerParams(dimension_semantics=("arbitrary",), vmem_limit_bytes=PEER_VMEM)
    stream_scratch = [pltpu.SMEM((PEER_NT * P_PAIRS,), jnp.int32), pltpu.SMEM((PEER_NT * P_PAIRS,), jnp.int32),
                      pltpu.SemaphoreType.DMA((2,)),
                      pltpu.VMEM((4 * P_PAIRS, 128), jnp.int32), pltpu.VMEM((4 * P_PAIRS, 128), jnp.int32)]
    w = pl.pallas_call(
        _peer_act_kernel,
        grid=grid,
        in_specs=[idx_spec,
                  pl.BlockSpec((tb, d), lambda i: (i, 0)),
                  pl.BlockSpec((tb, P_PAIRS), lambda i: (i, 0)),
                  tab_spec],
        out_specs=pl.BlockSpec((tb, P_PAIRS), lambda i: (i, 0)),
        out_shape=jax.ShapeDtypeStruct((t, P_PAIRS), F32),
        scratch_shapes=stream_scratch + [pltpu.VMEM((tb, 4 * P_PAIRS), F32), pltpu.VMEM((tb, 4 * P_PAIRS), F32)],
        compiler_params=params,
        name="peer_act",
    )(idx_chunks, h, gate, u_pack)
    return pl.pallas_call(
        functools.partial(_peer_out_kernel, final=final),
        grid=grid,
        in_specs=[idx_spec,
                  pl.BlockSpec((tb, P_PAIRS), lambda i: (i, 0)),
                  tab_spec,
                  pl.BlockSpec((tb, d), lambda i: (i, 0)),
                  pl.BlockSpec((1, 6, d), lambda i: (mod_of_block(i), 0, 0)),
                  _full((1, d))],
        out_specs=pl.BlockSpec((tb, d), lambda i: (i, 0)),
        out_shape=jax.ShapeDtypeStruct((t, d), F32),
        scratch_shapes=stream_scratch + [pltpu.VMEM((tb, 8 * P_PAIRS), F32), pltpu.VMEM((tb, 8 * P_PAIRS), F32),
                                         pltpu.VMEM((tb, d), F32)],
        compiler_params=params,
        name="peer_out",
    )(idx_chunks, w, v_pack, x, mods, norm_f.reshape(1, d))


def _peer_layer(x3, norm2, mods, mod_index, w_q, keys, u_tab, v_tab, norm_f, final):
    b, l, d = x3.shape
    q, h = _norm_mod_proj(x3, norm2, mods, mod_index, 3, w_q, highest=True, emit_h=True, name="peer_query")
    idx_t, gate_t = _peer_topk(q.reshape(b * l, d), keys)
    per_seq = l // (2 * PEER_NT)
    mod_of_block = lambda i: mod_index(i // per_seq, (i % per_seq) // (TM // (2 * PEER_NT)))
    y = _peer_experts(x3.reshape(b * l, d), h.reshape(b * l, d), idx_t.T, gate_t.T,
                      _pack_table(u_tab), _pack_table(v_tab), mods, mod_of_block, norm_f, final)
    return y.reshape(b, l, d)


def kernel(x, c, ctx, c_ctx, w_mod, b_mod, norm1, norm2, ev_w_in, ev_a_conv, ev_qkv_conv, ev_a_log, ev_dt_bias, ev_o_norm, ev_w_out, od_w_in, od_q_norm, od_kv_norm, od_w_uq, od_w_ukv, od_w_out, p_w_q, p_keys, p_u, p_v, norm_f):
    b, s, d = x.shape
    assert ctx.shape[1] == TM and s % TM == 0
    c_rows = jnp.concatenate([c, c_ctx[None, :], jnp.zeros((16 - b - 1, d), F32)], axis=0)
    seq = jnp.concatenate([ctx, x], axis=1)
    joint = lambda bi, i: jnp.where(i == 0, b, bi)
    latent = lambda bi, i: bi

    mods = _modulation(c_rows, w_mod[0], b_mod[0])
    w_in = ev_w_in[0]
    proj, ab_t, edges = _norm_mod_proj(seq, norm1[0], mods, joint, 0, w_in[:, :3584].astype(BF16),
                                       w_t=w_in[:, 3584:].T, name="even_in")
    y_a, qkv = _even_prep(proj, edges, ev_a_conv[0], ev_qkv_conv[0])
    o_f, o_b = _gdn(qkv, ab_t, ev_a_log[0], ev_dt_bias[0])
    seq = _even_out(o_f, o_b, proj, y_a, seq, mods, joint, ev_o_norm[0], ev_w_out[0])
    seq = _peer_layer(seq, norm2[0], mods, joint, p_w_q[0], p_keys[0], p_u[0], p_v[0], norm_f, False)

    mods = _modulation(c_rows, w_mod[1], b_mod[1])
    w_in = od_w_in[0]
    k_rope = w_in[:, 640:704]
    w_ext = jnp.concatenate([w_in[:, :640], _pad_lanes(k_rope), _pad_lanes(_swap_cols(k_rope))], axis=1)
    (proj,) = _norm_mod_proj(seq, norm1[1], mods, joint, 0, w_ext.astype(BF16), name="mla_in")
    cos, sin = _rope_tables(s)
    q, k, v = _mla_expand(proj, od_q_norm[0], od_kv_norm[0], od_w_uq[0], od_w_ukv[0], cos, sin)
    o = _attention(q, k, v)
    xs = _attn_out(o, seq, mods, od_w_out[0])
    return _peer_layer(xs, norm2[1], mods, latent, p_w_q[1], p_keys[1], p_u[1], p_v[1], norm_f, True)
```

```python
import functools
import math

import jax
import jax.numpy as jnp
from jax import lax
from jax.experimental import pallas as pl
from jax.experimental.pallas import tpu as pltpu

F32 = jnp.float32
BF16 = jnp.bfloat16
HIGHEST = lax.Precision.HIGHEST

EPS = 1e-6
GRID_W = 64
ROPE_BASE = 10000.0
TM = 256
GDN_C = 128
GDN_HEADS = 4
C_HEADS = 8
C_NOPE = 128
C_ROPE = 64
P_HEADS = 8
P_NKEYS = 128
P_TOPK = 16
P_PAIRS = P_HEADS * P_TOPK
PEER_NT = 128
TOPK_TB = 256
ATTN_TQ = 512
ATTN_PARTS = 2
VMEM_LIMIT = 48 * 1024 * 1024
PEER_VMEM = 50 * 1024 * 1024


def _nt_dot(a, b, precision=None):
    return lax.dot_general(a, b, (((1,), (1,)), ((), ())), precision=precision, preferred_element_type=F32)


def _tn_dot(a, b, precision=None):
    return lax.dot_general(a, b, (((0,), (0,)), ((), ())), precision=precision, preferred_element_type=F32)


def _dot(a, b, precision=None):
    return jnp.dot(a, b, precision=precision, preferred_element_type=F32)


_NN = (((1,), (0,)), ((), ()))
_NT = (((1,), (1,)), ((), ()))
_TN = (((0,), (0,)), ((), ()))


def _split2(x):
    hi = x.astype(BF16)
    return hi, (x - hi.astype(F32)).astype(BF16)


def _mm3(a, b, dims):
    f = lambda x, y: lax.dot_general(x, y, dims, preferred_element_type=F32)
    return f(a[0], b[0]) + (f(a[0], b[1]) + f(a[1], b[0]))


def _mm_exact(x, m, dims):
    hi = x.astype(BF16)
    r = x - hi.astype(F32)
    mid = r.astype(BF16)
    lo = (r - mid.astype(F32)).astype(BF16)
    f = lambda y: lax.dot_general(y, m, dims, preferred_element_type=F32)
    return f(hi) + (f(mid) + f(lo))


def _silu(x):
    return x * jax.nn.sigmoid(x)


def _params(*sem):
    return pltpu.CompilerParams(dimension_semantics=sem, vmem_limit_bytes=VMEM_LIMIT)


def _full(shape):
    return pl.BlockSpec(shape, lambda *_: (0,) * len(shape))


def _mod_kernel(c_ref, w_ref, b_ref, o_ref):
    o_ref[...] = _dot(_silu(c_ref[...]), w_ref[...], HIGHEST) + b_ref[...]


def _modulation(c_rows, w, b):
    r, d = c_rows.shape
    n = w.shape[1]
    tn = 512
    out = pl.pallas_call(
        _mod_kernel,
        grid=(n // tn,),
        in_specs=[_full((r, d)), pl.BlockSpec((d, tn), lambda j: (0, j)), pl.BlockSpec((1, tn), lambda j: (0, j))],
        out_specs=pl.BlockSpec((r, tn), lambda j: (0, j)),
        out_shape=jax.ShapeDtypeStruct((r, n), F32),
        compiler_params=_params("arbitrary"),
        name="modulation",
    )(c_rows, w, b.reshape(1, n))
    return out.reshape(r, 6, d)


def _proj_kernel(*refs, shift_row, highest, has_wt, emit_h):
    x_ref, g_ref, m_ref, w_ref = refs[:4]
    rest = list(refs[4:])
    wt_ref = rest.pop(0) if has_wt else None
    o_ref = rest.pop(0)
    x = x_ref[0]
    h = x * lax.rsqrt(jnp.mean(x * x, axis=-1, keepdims=True) + EPS) * g_ref[...]
    h = h * (1.0 + m_ref[0, shift_row + 1:shift_row + 2, :]) + m_ref[0, shift_row:shift_row + 1, :]
    if highest:
        o_ref[0] = _mm3(_split2(h), (w_ref[0], w_ref[1]), _NN).astype(o_ref.dtype)
    else:
        o_ref[0] = _dot(h.astype(BF16), w_ref[...]).astype(o_ref.dtype)
    if has_wt:
        rest.pop(0)[0] = _nt_dot(wt_ref[...], h, HIGHEST)
        edge_ref = rest.pop(0)
        edge_ref[0, 0, 0:1, :] = o_ref[0, 0:1, :]
        edge_ref[0, 0, 1:2, :] = o_ref[0, TM - 1:TM, :]
    if emit_h:
        rest.pop(0)[0] = h


def _norm_mod_proj(x3, gain, mods, mod_index, shift_row, w, *, highest=False, w_t=None, emit_h=False, name):
    b, l, d = x3.shape
    n = w.shape[1]
    if highest:
        w = jnp.stack(_split2(w))
    grid = (b, l // TM)
    in_specs = [pl.BlockSpec((1, TM, d), lambda bi, i: (bi, i, 0)),
                _full((1, d)),
                pl.BlockSpec((1, 6, d), lambda bi, i: (mod_index(bi, i), 0, 0)),
                _full(w.shape)]
    args = [x3, gain.reshape(1, d), mods, w]
    out_specs = [pl.BlockSpec((1, TM, n), lambda bi, i: (bi, i, 0))]
    out_shape = [jax.ShapeDtypeStruct((b, l, n), F32)]
    if w_t is not None:
        in_specs.append(_full(w_t.shape))
        args.append(w_t)
        out_specs.append(pl.BlockSpec((1, w_t.shape[0], TM), lambda bi, i: (bi, 0, i)))
        out_shape.append(jax.ShapeDtypeStruct((b, w_t.shape[0], l), F32))
        out_specs.append(pl.BlockSpec((1, 1, 2, n), lambda bi, i: (bi, i, 0, 0)))
        out_shape.append(jax.ShapeDtypeStruct((b, l // TM, 2, n), F32))
    if emit_h:
        out_specs.append(pl.BlockSpec((1, TM, d), lambda bi, i: (bi, i, 0)))
        out_shape.append(jax.ShapeDtypeStruct((b, l, d), F32))
    return pl.pallas_call(
        functools.partial(_proj_kernel, shift_row=shift_row, highest=highest, has_wt=w_t is not None, emit_h=emit_h),
        grid=grid, in_specs=in_specs, out_specs=out_specs, out_shape=out_shape,
        compiler_params=_params("arbitrary", "arbitrary"), name=name,
    )(*args)


def _even_prep_kernel(p_ref, halo_ref, aconv_ref, qconv_ref, ya_ref, qkv_ref):
    tm = p_ref.shape[1]
    rows = lax.broadcasted_iota(jnp.int32, (tm, 512), 0)
    first = rows == 0
    last = rows == tm - 1

    def conv(x, x_prev, x_next, w):
        xm = jnp.where(first, x_prev, pltpu.roll(x, 1, 0))
        xp = jnp.where(last, x_next, pltpu.roll(x, tm - 1, 0))
        return xm * w[0:1] + x * w[1:2] + xp * w[2:3]

    def cols(k):
        return p_ref[0, :, 512 * k:512 * (k + 1)]

    def halo(r, k):
        return halo_ref[0, 0, r:r + 1, 512 * k:512 * (k + 1)]

    ca = conv(cols(1) * cols(2), halo(0, 1) * halo(0, 2), halo(1, 1) * halo(1, 2), aconv_ref[...])
    ya_ref[0] = cols(0) * ca
    for s in range(3):
        y = _silu(conv(cols(3 + s), halo(0, 3 + s), halo(1, 3 + s), qconv_ref[:, 512 * s:512 * (s + 1)]))
        if s < 2:
            parts = []
            for h in range(GDN_HEADS):
                t = y[:, 128 * h:128 * (h + 1)]
                parts.append(t * lax.rsqrt(jnp.sum(t * t, axis=-1, keepdims=True) + EPS))
            y = jnp.concatenate(parts, axis=-1)
        qkv_ref[0, :, 512 * s:512 * (s + 1)] = y


def _even_prep(proj, edges, a_conv, qkv_conv):
    b, l, n = proj.shape
    nb = l // TM
    firstrows = edges[:, :, 0, :]
    lastrows = edges[:, :, 1, :]
    blk = jnp.arange(nb)
    prev = jnp.where((blk >= 2)[None, :, None], jnp.roll(lastrows, 1, axis=1), 0.0)
    nxt = jnp.where(((blk >= 1) & (blk < nb - 1))[None, :, None], jnp.roll(firstrows, -1, axis=1), 0.0)
    halo = jnp.stack([prev, nxt], axis=2)
    return pl.pallas_call(
        _even_prep_kernel,
        grid=(b, nb),
        in_specs=[pl.BlockSpec((1, TM, 3072), lambda bi, i: (bi, i, 0)),
                  pl.BlockSpec((1, 1, 2, n), lambda bi, i: (bi, i, 0, 0)),
                  _full(a_conv.shape), _full(qkv_conv.shape)],
        out_specs=[pl.BlockSpec((1, TM, 512), lambda bi, i: (bi, i, 0)),
                   pl.BlockSpec((1, TM, 1536), lambda bi, i: (bi, i, 0))],
        out_shape=[jax.ShapeDtypeStruct((b, l, 512), F32), jax.ShapeDtypeStruct((b, l, 1536), F32)],
        compiler_params=_params("arbitrary", "arbitrary"), name="even_prep",
    )(proj, halo, a_conv, qkv_conv)


def _gdn_kernel(qf_ref, qb_ref, abf_ref, abb_ref, alog_ref, dtb_ref, of_ref, ob_ref, s_ref):
    n = pl.program_id(1)
    c = GDN_C

    @pl.when(n == 0)
    def _():
        s_ref[...] = jnp.zeros_like(s_ref)

    row = lax.broadcasted_iota(jnp.int32, (c, c), 0)
    col = lax.broadcasted_iota(jnp.int32, (c, c), 1)
    eye = row == col
    eyef = jnp.where(eye, 1.0, 0.0)
    same = {b: (row // b) == (col // b) for b in (8, 16, 32, 64)}
    same[2 * 64] = row >= 0
    merge = {b: same[2 * b] & jnp.logical_not(same[b]) for b in (8, 16, 32, 64)}
    ones = jnp.ones((c, c), BF16)
    scale = 128 ** -0.5
    x_refs = (qf_ref, qb_ref)
    o_refs = (of_ref, ob_ref)
    tri = ((row >= col), (row <= col))
    strict = ((row > col), (row < col))
    trif = [jnp.where(m, 1.0, 0.0).astype(BF16) for m in tri]
    g4, beta4 = [], []
    for d, ab_ref in enumerate((abf_ref, abb_ref)):
        ab = ab_ref[0]
        g4.append(-jnp.exp(alog_ref[4 * d:4 * d + 4, :])
                  * jax.nn.softplus(ab[4 * d:4 * d + 4, :] + dtb_ref[4 * d:4 * d + 4, :]))
        beta4.append(jax.nn.sigmoid(ab[8 + 4 * d:12 + 4 * d, :]))
    chain = [(d, h) for d in range(2) for h in range(GDN_HEADS)]
    ids = range(len(chain))
    gt = [jnp.broadcast_to(g4[d][h:h + 1, :], (c, c)) for d, h in chain]
    bt = [jnp.broadcast_to(beta4[d][h:h + 1, :], (c, c)) for d, h in chain]
    dcol = [_mm_exact(jnp.where(tri[chain[i][0]], gt[i], 0.0), ones, _NN) for i in ids]
    drow = [_mm_exact(gt[i], trif[chain[i][0]], _NT) for i in ids]
    tot = [_mm_exact(gt[i], ones, _NN) for i in ids]
    bc = [_mm_exact(jnp.where(eye, bt[i], 0.0), ones, _NN) for i in ids]
    gamma = [jnp.exp(jnp.where(tri[chain[i][0]], dcol[i] - drow[i], -jnp.inf)) for i in ids]
    edec = [jnp.exp(dcol[i]) for i in ids]
    q = [x_refs[d][0, :, 128 * h:128 * (h + 1)] * scale for d, h in chain]
    k = [x_refs[d][0, :, 512 + 128 * h:512 + 128 * (h + 1)] for d, h in chain]
    v = [x_refs[d][0, :, 1024 + 128 * h:1024 + 128 * (h + 1)] for d, h in chain]
    kb = [k[i] * bc[i] for i in ids]
    k2 = [_split2(k[i]) for i in ids]
    a = [jnp.where(strict[chain[i][0]], _mm3(_split2(kb[i]), k2[i], _NT) * gamma[i], 0.0) for i in ids]
    d8 = [jnp.where(same[8], a[i], 0.0) for i in ids]
    t = [eyef - d8[i] for i in ids]
    p2 = [_split2(d8[i]) for i in ids]
    for _ in range(2):
        p2 = [_split2(_mm3(p2[i], p2[i], _NN)) for i in ids]
        t = [t[i] + _mm3(_split2(t[i]), p2[i], _NN) for i in ids]
    for blk in (8, 16, 32, 64):
        off = [jnp.where(merge[blk], a[i], 0.0) for i in ids]
        t2 = [_split2(t[i]) for i in ids]
        lt = [_mm3(_split2(off[i]), t2[i], _NN) for i in ids]
        t = [t[i] - _mm3(t2[i], _split2(lt[i]), _NN) for i in ids]
    t2 = [_split2(t[i]) for i in ids]
    u = [_mm3(t2[i], _split2(v[i] * bc[i]), _NN) for i in ids]
    w = [_mm3(t2[i], _split2(kb[i] * edec[i]), _NN) for i in ids]
    intra = [jnp.where(tri[chain[i][0]], _mm3(_split2(q[i]), k2[i], _NT) * gamma[i], 0.0) for i in ids]
    s = [s_ref[i] for i in ids]
    s2 = [_split2(s[i]) for i in ids]
    v_new = [u[i] - _mm3(_split2(w[i]), s2[i], _NN) for i in ids]
    v2 = [_split2(v_new[i]) for i in ids]
    for i, (d, h) in enumerate(chain):
        o_refs[d][0, :, 128 * h:128 * (h + 1)] = (_mm3(_split2(q[i] * edec[i]), s2[i], _NN)
                                                  + _mm3(_split2(intra[i]), v2[i], _NN))
        s_ref[i] = s[i] * jnp.exp(tot[i]) + _mm3(_split2(k[i] * jnp.exp(tot[i] - dcol[i])), v2[i], _TN)


def _gdn(qkv, ab_t, a_log, dt_bias):
    b, l, _ = qkv.shape
    nb = l // GDN_C
    nctx = TM // GDN_C

    def bwd(n):
        return jnp.where(n < nctx, nctx - 1 - n, nb + nctx - 1 - n)

    alog = jnp.broadcast_to(a_log.reshape(8, 1), (8, 128))
    dtb = jnp.broadcast_to(dt_bias.reshape(8, 1), (8, 128))
    return pl.pallas_call(
        _gdn_kernel,
        grid=(b, nb),
        in_specs=[pl.BlockSpec((1, GDN_C, 1536), lambda bi, n: (bi, n, 0)),
                  pl.BlockSpec((1, GDN_C, 1536), lambda bi, n: (bi, bwd(n), 0)),
                  pl.BlockSpec((1, 16, GDN_C), lambda bi, n: (bi, 0, n)),
                  pl.BlockSpec((1, 16, GDN_C), lambda bi, n: (bi, 0, bwd(n))),
                  _full((8, 128)), _full((8, 128))],
        out_specs=[pl.BlockSpec((1, GDN_C, 512), lambda bi, n: (bi, n, 0)),
                   pl.BlockSpec((1, GDN_C, 512), lambda bi, n: (bi, bwd(n), 0))],
        out_shape=[jax.ShapeDtypeStruct((b, l, 512), F32), jax.ShapeDtypeStruct((b, l, 512), F32)],
        scratch_shapes=[pltpu.VMEM((2 * GDN_HEADS, 128, 128), F32)],
        compiler_params=_params("arbitrary", "arbitrary"), name="gdn",
    )(qkv, qkv, ab_t, ab_t, alog, dtb)


def _even_out_kernel(of_ref, ob_ref, z_ref, ya_ref, x_ref, m_ref, on_ref, w_ref, o_ref):
    o = of_ref[0] + ob_ref[0]
    z = z_ref[0]
    parts = []
    for h in range(GDN_HEADS):
        oh = o[:, 128 * h:128 * (h + 1)]
        y = oh * lax.rsqrt(jnp.mean(oh * oh, axis=-1, keepdims=True) + EPS) * on_ref[...]
        parts.append(y * _silu(z[:, 128 * h:128 * (h + 1)]))
    yb = jnp.concatenate(parts, axis=-1)
    y = _dot(ya_ref[0].astype(BF16), w_ref[0:512, :]) + _dot(yb.astype(BF16), w_ref[512:1536, :])
    o_ref[0] = x_ref[0] + m_ref[0, 2:3, :] * y


def _even_out(o_f, o_b, proj, y_a, x3, mods, mod_index, o_norm, w_out):
    b, l, d = x3.shape
    tok = lambda n: pl.BlockSpec((1, TM, n), lambda bi, i: (bi, i, 0))
    return pl.pallas_call(
        _even_out_kernel,
        grid=(b, l // TM),
        in_specs=[tok(512), tok(512),
                  pl.BlockSpec((1, TM, 512), lambda bi, i: (bi, i, 6)),
                  tok(512), tok(d),
                  pl.BlockSpec((1, 6, d), lambda bi, i: (mod_index(bi, i), 0, 0)),
                  _full((1, 128)), _full(w_out.shape)],
        out_specs=tok(d),
        out_shape=jax.ShapeDtypeStruct((b, l, d), F32),
        compiler_params=_params("arbitrary", "arbitrary"), name="even_out",
    )(o_f, o_b, proj, y_a, x3, mods, o_norm.reshape(1, 128), w_out.astype(BF16))


def _rope_tables(s):
    t = jnp.arange(s, dtype=jnp.int32)
    n = C_ROPE // 4
    freqs = ROPE_BASE ** (-jnp.arange(n, dtype=F32) / n)

    def seg(pos):
        ang = pos.astype(F32)[:, None] * freqs[None, :]
        return (jnp.concatenate([jnp.cos(ang), jnp.cos(ang)], -1), jnp.concatenate([-jnp.sin(ang), jnp.sin(ang)], -1))

    cr, sr = seg(t // GRID_W)
    cc, sc = seg(t % GRID_W)
    cos = jnp.concatenate([jnp.ones((TM, C_ROPE), F32), jnp.concatenate([cr, cc], -1)], 0)
    sin = jnp.concatenate([jnp.zeros((TM, C_ROPE), F32), jnp.concatenate([sr, sc], -1)], 0)
    return cos, sin


def _swap_cols(w):
    n = C_ROPE // 4
    perm = jnp.concatenate([jnp.arange(n) + n, jnp.arange(n), jnp.arange(n) + 3 * n, jnp.arange(n) + 2 * n])
    return w[..., perm]


def _pad_lanes(w, width=128):
    return jnp.concatenate([w, jnp.zeros(w.shape[:-1] + (width - w.shape[-1],), w.dtype)], axis=-1)


def _mla_expand_kernel(p_ref, qn_ref, kvn_ref, cos_ref, sin_ref, wq_ref, wqs_ref, wk_ref, wv_ref,
                       oq_ref, ok_ref, ov_ref):
    def norm(x, g):
        return (x * lax.rsqrt(jnp.mean(x * x, axis=-1, keepdims=True) + EPS) * g).astype(BF16)

    cq = norm(p_ref[0, :, 0:384], qn_ref[...])
    ckv = norm(p_ref[0, :, 384:640], kvn_ref[...])
    cos = cos_ref[...]
    sin = sin_ref[...]
    scale = (C_NOPE + C_ROPE) ** -0.5
    cos_h = jnp.concatenate([jnp.ones_like(cos), cos] * C_HEADS, axis=-1)
    sin_h = jnp.concatenate([jnp.zeros_like(sin), sin] * C_HEADS, axis=-1)
    oq_ref[0] = ((_dot(cq, wq_ref[...]) * cos_h + _dot(cq, wqs_ref[...]) * sin_h) * scale).astype(BF16)
    kn = _dot(ckv, wk_ref[...]).astype(BF16)
    kr = (p_ref[0, :, 640:768] * cos + p_ref[0, :, 768:896] * sin).astype(BF16)
    parts = []
    for h in range(C_HEADS):
        parts += [kn[:, 128 * h:128 * (h + 1)], kr]
    ok_ref[0] = jnp.concatenate(parts, axis=-1)
    ov_ref[0] = _dot(ckv, wv_ref[...]).astype(BF16)


def _mla_expand(proj, q_norm, kv_norm, w_uq, w_ukv, cos, sin):
    b, l, n = proj.shape
    hd = C_HEADS * 128
    wq = w_uq.reshape(-1, C_HEADS, C_NOPE + C_ROPE)
    rope = wq[:, :, C_NOPE:]
    w_q = jnp.concatenate([wq[:, :, :C_NOPE], _pad_lanes(rope)], axis=-1).reshape(-1, 2 * hd).astype(BF16)
    w_qs = jnp.concatenate([jnp.zeros_like(wq[:, :, :C_NOPE]), _pad_lanes(_swap_cols(rope))],
                           axis=-1).reshape(-1, 2 * hd).astype(BF16)
    wkv = w_ukv.reshape(-1, C_HEADS, 2 * 128)
    wk = wkv[:, :, :128].reshape(-1, hd).astype(BF16)
    wv = wkv[:, :, 128:].reshape(-1, hd).astype(BF16)
    tok = lambda m: pl.BlockSpec((1, TM, m), lambda bi, i: (bi, i, 0))
    tab = pl.BlockSpec((TM, 128), lambda bi, i: (i, 0))
    return pl.pallas_call(
        _mla_expand_kernel,
        grid=(b, l // TM),
        in_specs=[tok(n), _full((1, 384)), _full((1, 256)), tab, tab,
                  _full(w_q.shape), _full(w_qs.shape), _full(wk.shape), _full(wv.shape)],
        out_specs=[tok(2 * hd), tok(2 * hd), tok(hd)],
        out_shape=[jax.ShapeDtypeStruct((b, l, 2 * hd), BF16), jax.ShapeDtypeStruct((b, l, 2 * hd), BF16),
                   jax.ShapeDtypeStruct((b, l, hd), BF16)],
        compiler_params=_params("arbitrary", "arbitrary"), name="mla_expand",
    )(proj, q_norm.reshape(1, -1), kv_norm.reshape(1, -1), _pad_lanes(cos), _pad_lanes(sin), w_q, w_qs, wk, wv)


def _attn_kernel(*refs):
    q_refs, (k_ref, v_ref, o_ref) = refs[:ATTN_PARTS], refs[ATTN_PARTS:]
    rows = q_refs[0].shape[1]
    s = [_nt_dot(q[0], k_ref[0]) for q in q_refs]
    m = [jnp.max(x, axis=-1, keepdims=True) for x in s]
    p = [jnp.exp(x - y) for x, y in zip(s, m)]
    den = [jnp.sum(x, axis=-1, keepdims=True) for x in p]
    o = [_dot(x.astype(BF16), v_ref[0]) for x in p]
    for j, (x, y) in enumerate(zip(o, den)):
        o_ref[0, j * rows:(j + 1) * rows, :] = (x / y).astype(o_ref.dtype)


def _attention(q, k, v):
    b, l, hd = v.shape
    s = l - TM
    rows = ATTN_TQ // ATTN_PARTS
    first = TM // rows

    def part(j):
        return pl.BlockSpec((1, rows, 256), lambda bi, h, i: (bi, ATTN_PARTS * i + first + j, h))

    return pl.pallas_call(
        _attn_kernel,
        grid=(b, C_HEADS, s // ATTN_TQ),
        in_specs=[part(j) for j in range(ATTN_PARTS)]
        + [pl.BlockSpec((1, l, 256), lambda bi, h, i: (bi, 0, h)),
           pl.BlockSpec((1, l, 128), lambda bi, h, i: (bi, 0, h))],
        out_specs=pl.BlockSpec((1, ATTN_TQ, 128), lambda bi, h, i: (bi, i, h)),
        out_shape=jax.ShapeDtypeStruct((b, s, hd), BF16),
        compiler_params=_params("arbitrary", "arbitrary", "arbitrary"), name="attention",
    )(*([q] * ATTN_PARTS), k, v)


def _attn_out_kernel(o_ref, x_ref, m_ref, w_ref, out_ref):
    out_ref[0] = x_ref[0] + m_ref[0, 2:3, :] * _dot(o_ref[0], w_ref[...])


def _attn_out(o, x3, mods, w_out):
    b, s, d = o.shape[0], o.shape[1], x3.shape[2]
    return pl.pallas_call(
        _attn_out_kernel,
        grid=(b, s // TM),
        in_specs=[pl.BlockSpec((1, TM, o.shape[2]), lambda bi, i: (bi, i, 0)),
                  pl.BlockSpec((1, TM, d), lambda bi, i: (bi, i + 1, 0)),
                  pl.BlockSpec((1, 6, d), lambda bi, i: (bi, 0, 0)),
                  _full(w_out.shape)],
        out_specs=pl.BlockSpec((1, TM, d), lambda bi, i: (bi, i, 0)),
        out_shape=jax.ShapeDtypeStruct((b, s, d), F32),
        compiler_params=_params("arbitrary", "arbitrary"), name="attn_out",
    )(o, x3, mods, w_out.astype(BF16))


def _topk_rows(s, k, ids=None):
    if ids is None:
        ids = lax.broadcasted_iota(jnp.int32, s.shape, 0)
    big = jnp.iinfo(jnp.int32).max
    vals, idxs = [], []
    for _ in range(k):
        m = jnp.max(s, axis=0, keepdims=True)
        am = jnp.min(jnp.where(s == m, ids, big), axis=0, keepdims=True)
        vals.append(m)
        idxs.append(am)
        s = jnp.where(ids == am, -jnp.inf, s)
    return vals, idxs


def _candidates(v1, v2):
    tl = v1[0].shape[1]
    v2s = jnp.concatenate(v2, axis=0)
    sub8 = lax.broadcasted_iota(jnp.int32, (8, tl), 0)
    sub16 = lax.broadcasted_iota(jnp.int32, (P_TOPK, tl), 0)
    vals = [v1[0] + v2s]
    ids = [sub16]
    for a in range(1, 5):
        nb = P_TOPK // (a + 1)
        vals.append(jnp.where(sub8 < nb, v1[a] + v2s[:8], -jnp.inf))
        ids.append(sub8 + P_TOPK * a)
    rest = [(a, b) for a in range(5, P_TOPK) for b in range(P_TOPK // (a + 1))]
    for group in (rest[:8], rest[8:]):
        rows = [v1[a] + v2[b] for a, b in group]
        rid = [jnp.full((1, tl), P_TOPK * a + b, jnp.int32) for a, b in group]
        pad = 8 - len(group)
        if pad:
            rows.append(jnp.full((pad, tl), -jnp.inf, F32))
            rid.append(jnp.full((pad, tl), P_TOPK * P_TOPK, jnp.int32))
        vals.append(jnp.concatenate(rows, axis=0))
        ids.append(jnp.concatenate(rid, axis=0))
    return jnp.concatenate(vals, axis=0), jnp.concatenate(ids, axis=0)


def _peer_topk_kernel(q_ref, key_ref, idx_ref, gate_ref):
    tl = 128
    for sub in range(q_ref.shape[0] // tl):
        for h in range(P_HEADS):
            q = q_ref[sub * tl:(sub + 1) * tl, 128 * h:128 * (h + 1)]
            st = _nt_dot(key_ref[h], q, HIGHEST)
            v1, i1 = _topk_rows(st[:P_NKEYS], P_TOPK)
            v2, i2 = _topk_rows(st[P_NKEYS:], P_TOPK)
            cand, cand_pos = _candidates(v1, v2)
            sc, pos = _topk_rows(cand, P_TOPK, cand_pos)
            sc = jnp.concatenate(sc, axis=0)
            pos = jnp.concatenate(pos, axis=0)
            pa = pos >> 4
            pb = pos & (P_TOPK - 1)
            e1 = jnp.zeros_like(pos)
            e2 = jnp.zeros_like(pos)
            for r in range(P_TOPK):
                e1 = jnp.where(pa == r, i1[r], e1)
                e2 = jnp.where(pb == r, i2[r], e2)
            ex = jnp.exp(sc - sc[0:1])
            idx_ref[P_TOPK * h:P_TOPK * (h + 1), sub * tl:(sub + 1) * tl] = (e1 * P_NKEYS + e2) * 4
            gate_ref[P_TOPK * h:P_TOPK * (h + 1), sub * tl:(sub + 1) * tl] = ex / jnp.sum(ex, axis=0, keepdims=True)


def _peer_topk(q, keys):
    t = q.shape[0]
    z = jnp.zeros((P_HEADS, P_NKEYS, 64), F32)
    kbd = jnp.concatenate([jnp.concatenate([keys[:, 0], z], -1), jnp.concatenate([z, keys[:, 1]], -1)], axis=1)
    return pl.pallas_call(
        _peer_topk_kernel,
        grid=(t // TOPK_TB,),
        in_specs=[pl.BlockSpec((TOPK_TB, 1024), lambda i: (i, 0)), _full(kbd.shape)],
        out_specs=[pl.BlockSpec((P_PAIRS, TOPK_TB), lambda i: (0, i)), pl.BlockSpec((P_PAIRS, TOPK_TB), lambda i: (0, i))],
        out_shape=[jax.ShapeDtypeStruct((P_PAIRS, t), jnp.int32), jax.ShapeDtypeStruct((P_PAIRS, t), F32)],
        compiler_params=_params("arbitrary"), name="peer_topk",
    )(q, kbd)


def _pack_kernel(x_ref, o_ref):
    eb = x_ref.shape[0]
    for r in range(4):
        halves = []
        for k in range(2):
            b = pltpu.bitcast(x_ref[:, 256 * r + 128 * k:256 * r + 128 * (k + 1)], jnp.uint32)
            halves.append((b + jnp.uint32(0x7FFF) + ((b >> 16) & jnp.uint32(1))) >> 16)
        o_ref[pl.ds(r, eb, stride=4), :] = pltpu.bitcast(halves[0] | (halves[1] << 16), jnp.int32)


def _pack_table(tab):
    e = tab.shape[0]
    eb = 256
    return pl.pallas_call(
        _pack_kernel,
        grid=(e // eb,),
        in_specs=[pl.BlockSpec((eb, 1024), lambda i: (i, 0))],
        out_specs=pl.BlockSpec((4 * eb, 128), lambda i: (i, 0)),
        out_shape=jax.ShapeDtypeStruct((4 * e, 128), jnp.int32),
        compiler_params=_params("arbitrary"), name="pack_table",
    )(tab)


def _gather_rows(idx_smem, tab_ref, tt, g_ref):
    for j in range(P_PAIRS):
        row = idx_smem[tt * P_PAIRS + j]
        g_ref[pl.ds(4 * j, 4), :] = tab_ref[pl.ds(pl.multiple_of(row, 4), 4), :]


def _index_stream(idx_hbm, bufs, sems, per_token):
    i = pl.program_id(0)
    n = pl.num_programs(0)

    def chunk(c, k):
        return pltpu.make_async_copy(idx_hbm.at[c], bufs[k], sems.at[k])

    @pl.when(i == 0)
    def _():
        chunk(0, 0).start()

    chunk(2 * i + 1, 1).start()
    chunk(2 * i, 0).wait()
    for tt in range(PEER_NT):
        per_token(tt, tt, bufs[0])

    @pl.when(i + 1 < n)
    def _():
        chunk(2 * i + 2, 0).start()

    chunk(2 * i + 1, 1).wait()
    for tt in range(PEER_NT):
        per_token(PEER_NT + tt, tt, bufs[1])


def _peer_act_kernel(idx_hbm, h_ref, gate_ref, tab_ref, w_ref, ia, ib, sems, g0, g1, rlo_ref, rhi_ref):
    half = 4 * P_PAIRS
    lane = lax.broadcasted_iota(jnp.int32, (16, half), 1)
    sub = lax.broadcasted_iota(jnp.int32, (16, half), 0)
    diag = (lane & 7) == (sub & 7)
    zero = jnp.zeros((8, 128), BF16)
    gbufs = (g0, g1)

    def per_token(t, tt, idx_smem):
        g_ref = gbufs[tt % 2]
        _gather_rows(idx_smem, tab_ref, tt, g_ref)
        gb = pltpu.bitcast(g_ref[...], BF16)
        g2 = jnp.concatenate([gb[:half], gb[half:]], axis=1)
        hb = jnp.concatenate([h_ref[t:t + 1, 128 * s:128 * (s + 1)] for s in range(8)], axis=0).astype(BF16)
        lhs = jnp.concatenate([jnp.concatenate([hb, zero], axis=1), jnp.concatenate([zero, hb], axis=1)], axis=0)
        r = jnp.where(diag, _nt_dot(lhs, g2), 0.0)
        rlo_ref[t:t + 1, :] = jnp.sum(r[:8], axis=0, keepdims=True)
        rhi_ref[t:t + 1, :] = jnp.sum(r[8:], axis=0, keepdims=True)

    _index_stream(idx_hbm, (ia, ib), sems, per_token)

    col = lax.broadcasted_iota(jnp.int32, (half, P_PAIRS), 0)
    pair = lax.broadcasted_iota(jnp.int32, (half, P_PAIRS), 1)

    def fold(r, first_pair):
        f = jnp.where((col >> 3) + first_pair == pair, 1.0, 0.0).astype(BF16)
        hi = r.astype(BF16)
        lo = (r - hi.astype(F32)).astype(BF16)
        return _dot(hi, f) + _dot(lo, f)

    act = fold(rlo_ref[...], 0) + fold(rhi_ref[...], P_PAIRS // 2)
    gelu = 0.5 * act * (1.0 + lax.erf(act * (1.0 / math.sqrt(2.0))))
    w_ref[...] = gate_ref[...] * gelu


def _peer_out_kernel(idx_hbm, w_ref, tab_ref, x_ref, m_ref, nf_ref, o_ref, ia, ib, sems, g0, g1,
                     whi_ref, wlo_ref, y_ref, *, final):
    lane = lax.broadcasted_iota(jnp.int32, (8, 8 * P_PAIRS), 1)
    sub = lax.broadcasted_iota(jnp.int32, (8, 8 * P_PAIRS), 0)
    diag = (lane & 7) == sub
    pair = lax.broadcasted_iota(jnp.int32, (P_PAIRS, 8 * P_PAIRS), 0)
    col = lax.broadcasted_iota(jnp.int32, (P_PAIRS, 8 * P_PAIRS), 1)
    spread = jnp.where((col >> 3) == pair, 1.0, 0.0).astype(BF16)
    w = w_ref[...]
    w_hi = w.astype(BF16)
    w_lo = (w - w_hi.astype(F32)).astype(BF16)
    whi_ref[...] = _dot(w_hi, spread)
    wlo_ref[...] = _dot(w_lo, spread)
    gbufs = (g0, g1)

    def per_token(t, tt, idx_smem):
        g_ref = gbufs[tt % 2]
        _gather_rows(idx_smem, tab_ref, tt, g_ref)
        gb = pltpu.bitcast(g_ref[...], BF16)
        rows = [jnp.where(diag, jnp.broadcast_to(r[t:t + 1, :], (8, 8 * P_PAIRS)), 0.0).astype(BF16)
                for r in (whi_ref, wlo_ref)]
        o = _dot(jnp.concatenate(rows, axis=0), gb)
        o = o[:8] + o[8:]
        for s in range(8):
            y_ref[t:t + 1, 128 * s:128 * (s + 1)] = o[s:s + 1, :]

    _index_stream(idx_hbm, (ia, ib), sems, per_token)

    y = x_ref[...] + m_ref[0, 5:6, :] * y_ref[...]
    if final:
        y = y * lax.rsqrt(jnp.mean(y * y, axis=-1, keepdims=True) + EPS) * nf_ref[...]
    o_ref[...] = y


def _peer_experts(x, h, idx4, gate, u_pack, v_pack, mods, mod_of_block, norm_f, final):
    t, d = h.shape
    tb = 2 * PEER_NT
    idx_chunks = idx4.reshape(t // PEER_NT, PEER_NT * P_PAIRS)
    grid = (t // tb,)
    idx_spec = pl.BlockSpec(memory_space=pl.ANY)
    tab_spec = pl.BlockSpec(memory_space=pltpu.VMEM)
    params = pltpu.CompilerParams(dimension_semantics=("arbitrary",), vmem_limit_bytes=PEER_VMEM)
    stream_scratch = [pltpu.SMEM((PEER_NT * P_PAIRS,), jnp.int32), pltpu.SMEM((PEER_NT * P_PAIRS,), jnp.int32),
                      pltpu.SemaphoreType.DMA((2,)),
                      pltpu.VMEM((4 * P_PAIRS, 128), jnp.int32), pltpu.VMEM((4 * P_PAIRS, 128), jnp.int32)]
    w = pl.pallas_call(
        _peer_act_kernel,
        grid=grid,
        in_specs=[idx_spec,
                  pl.BlockSpec((tb, d), lambda i: (i, 0)),
                  pl.BlockSpec((tb, P_PAIRS), lambda i: (i, 0)),
                  tab_spec],
        out_specs=pl.BlockSpec((tb, P_PAIRS), lambda i: (i, 0)),
        out_shape=jax.ShapeDtypeStruct((t, P_PAIRS), F32),
        scratch_shapes=stream_scratch + [pltpu.VMEM((tb, 4 * P_PAIRS), F32), pltpu.VMEM((tb, 4 * P_PAIRS), F32)],
        compiler_params=params,
        name="peer_act",
    )(idx_chunks, h, gate, u_pack)
    return pl.pallas_call(
        functools.partial(_peer_out_kernel, final=final),
        grid=grid,
        in_specs=[idx_spec,
                  pl.BlockSpec((tb, P_PAIRS), lambda i: (i, 0)),
                  tab_spec,
                  pl.BlockSpec((tb, d), lambda i: (i, 0)),
                  pl.BlockSpec((1, 6, d), lambda i: (mod_of_block(i), 0, 0)),
                  _full((1, d))],
        out_specs=pl.BlockSpec((tb, d), lambda i: (i, 0)),
        out_shape=jax.ShapeDtypeStruct((t, d), F32),
        scratch_shapes=stream_scratch + [pltpu.VMEM((tb, 8 * P_PAIRS), F32), pltpu.VMEM((tb, 8 * P_PAIRS), F32),
                                         pltpu.VMEM((tb, d), F32)],
        compiler_params=params,
        name="peer_out",
    )(idx_chunks, w, v_pack, x, mods, norm_f.reshape(1, d))


def _peer_layer(x3, norm2, mods, mod_index, w_q, keys, u_tab, v_tab, norm_f, final):
    b, l, d = x3.shape
    q, h = _norm_mod_proj(x3, norm2, mods, mod_index, 3, w_q, highest=True, emit_h=True, name="peer_query")
    idx_t, gate_t = _peer_topk(q.reshape(b * l, d), keys)
    per_seq = l // (2 * PEER_NT)
    mod_of_block = lambda i: mod_index(i // per_seq, (i % per_seq) // (TM // (2 * PEER_NT)))
    y = _peer_experts(x3.reshape(b * l, d), h.reshape(b * l, d), idx_t.T, gate_t.T,
                      _pack_table(u_tab), _pack_table(v_tab), mods, mod_of_block, norm_f, final)
    return y.reshape(b, l, d)


def kernel(x, c, ctx, c_ctx, w_mod, b_mod, norm1, norm2, ev_w_in, ev_a_conv, ev_qkv_conv, ev_a_log, ev_dt_bias, ev_o_norm, ev_w_out, od_w_in, od_q_norm, od_kv_norm, od_w_uq, od_w_ukv, od_w_out, p_w_q, p_keys, p_u, p_v, norm_f):
    b, s, d = x.shape
    assert ctx.shape[1] == TM and s % TM == 0
    c_rows = jnp.concatenate([c, c_ctx[None, :], jnp.zeros((16 - b - 1, d), F32)], axis=0)
    seq = jnp.concatenate([ctx, x], axis=1)
    joint = lambda bi, i: jnp.where(i == 0, b, bi)
    latent = lambda bi, i: bi

    mods = _modulation(c_rows, w_mod[0], b_mod[0])
    w_in = ev_w_in[0]
    proj, ab_t, edges = _norm_mod_proj(seq, norm1[0], mods, joint, 0, w_in[:, :3584].astype(BF16),
                                       w_t=w_in[:, 3584:].T, name="even_in")
    y_a, qkv = _even_prep(proj, edges, ev_a_conv[0], ev_qkv_conv[0])
    o_f, o_b = _gdn(qkv, ab_t, ev_a_log[0], ev_dt_bias[0])
    seq = _even_out(o_f, o_b, proj, y_a, seq, mods, joint, ev_o_norm[0], ev_w_out[0])
    seq = _peer_layer(seq, norm2[0], mods, joint, p_w_q[0], p_keys[0], p_u[0], p_v[0], norm_f, False)

    mods = _modulation(c_rows, w_mod[1], b_mod[1])
    w_in = od_w_in[0]
    k_rope = w_in[:, 640:704]
    w_ext = jnp.concatenate([w_in[:, :640], _pad_lanes(k_rope), _pad_lanes(_swap_cols(k_rope))], axis=1)
    (proj,) = _norm_mod_proj(seq, norm1[1], mods, joint, 0, w_ext.astype(BF16), name="mla_in")
    cos, sin = _rope_tables(s)
    q, k, v = _mla_expand(proj, od_q_norm[0], od_kv_norm[0], od_w_uq[0], od_w_ukv[0], cos, sin)
    o = _attention(q, k, v)
    xs = _attn_out(o, seq, mods, od_w_out[0])
    return _peer_layer(xs, norm2[1], mods, latent, p_w_q[1], p_keys[1], p_u[1], p_v[1], norm_f, True)
```
